```python
import math, functools
import jax, jax.numpy as jnp
from jax import lax
import numpy as np

D_MODEL = 1024
BATCH = 2
SEQ = 8192
DEPTH = 2

GRID_W = 64
CTX_LEN = 256
N_MIXERS = 2

HG_EXPAND = 128
HG_HEADS = D_MODEL // HG_EXPAND
HG_FK = HG_HEADS * HG_EXPAND
HG_DV = D_MODEL // HG_HEADS
HG_CHUNK = 64

RW_HEAD = 64
RW_HEADS = D_MODEL // RW_HEAD
RW_DECAY_LORA = 64
RW_ICLR_LORA = 64
RW_GATE_LORA = 128
RW_LNX_EPS = 1e-5 * RW_HEAD

N_EXPERTS = 32
TOP_K = 4
D_EXPERT = D_MODEL
SWIGLU_LIMIT = 7.0
SWIGLU_ALPHA = 1.702
MOE_BLOCK = 256

NORM_EPS = 1e-6

kernel_name = 'hybrid_hgrn2_rwkv7_moe_diffusion_trunk'


def _rmsnorm(x, g):
    xf = x.astype(jnp.float32)
    y = xf * lax.rsqrt(jnp.mean(xf * xf, axis=-1, keepdims=True) + NORM_EPS)
    return y.astype(x.dtype) * g


def _modulate(h, shift, scale):
    return h * (1 + scale) + shift


def _flip(a, rev):
    return jnp.flip(a, axis=1) if rev else a


def _gla_chunk_scan(q, k, v, logf, s0, with_output):
    bsz, t, nh, dk = k.shape
    dv = v.shape[-1]
    n = t // HG_CHUNK

    def chunks(a):
        return jnp.moveaxis(a.reshape(bsz, n, HG_CHUNK, nh, a.shape[-1]), 1, 0)

    causal = jnp.tril(jnp.ones((HG_CHUNK, HG_CHUNK), dtype=bool))[None, :, :, None, None]

    def step(state, inp):
        kc, vc, gc = inp[0], inp[1], inp[2]
        b = jnp.cumsum(gc, axis=1)
        b_end = b[:, -1]
        k_to_end = kc * jnp.exp(b_end[:, None] - b)
        state_new = jnp.exp(b_end)[..., None] * state + jnp.einsum('bshk,bshv->bhkv', k_to_end, vc)
        if not with_output:
            return state_new, None
        qc = inp[3]
        o_inter = jnp.einsum('bthk,bhkv->bthv', qc * jnp.exp(b), state)
        rel = jnp.exp(jnp.where(causal, b[:, :, None] - b[:, None], -jnp.inf))
        scores = jnp.einsum('bthk,bshk,btshk->bths', qc, kc, rel)
        o = o_inter + jnp.einsum('bths,bshv->bthv', scores, vc)
        return state_new, o

    seq = (k, v, logf) + ((q,) if with_output else ())
    state, o = lax.scan(step, s0, tuple(chunks(a) for a in seq))
    if with_output:
        o = jnp.moveaxis(o, 0, 1).reshape(bsz, t, nh, dv)
    return o, state


def _hgrn2_project(h, w_in, lb, readout):
    f32 = jnp.float32
    z = h @ w_in
    q, f_f, f_b, i, g = jnp.split(z, [HG_FK, 2 * HG_FK, 3 * HG_FK, 3 * HG_FK + D_MODEL], axis=-1)

    def heads(a, d):
        return a.reshape(a.shape[:2] + (HG_HEADS, d)).astype(f32)

    dirs = []
    for f, lbd in ((f_f, lb[0]), (f_b, lb[1])):
        fh = heads(f, HG_EXPAND)
        lbh = lbd.reshape(HG_HEADS, HG_EXPAND)
        logf = jnp.logaddexp(jnp.log(lbh), jnp.log1p(-lbh) + jax.nn.log_sigmoid(fh))
        kin = (1 - lbh) * jax.nn.sigmoid(-fh)
        dirs.append((kin, logf))
    qh = heads(jax.nn.silu(q), HG_EXPAND) if readout else None
    return qh, heads(i, HG_DV), (g if readout else None), dirs


def _hgrn2_readout(o, g, gnorm_w, w_out, dtype):
    gh = g.reshape(g.shape[:2] + (HG_HEADS, HG_DV)).astype(jnp.float32)
    o = _rmsnorm(o, gnorm_w.astype(jnp.float32)) * jax.nn.silu(gh)
    return o.reshape(o.shape[:2] + (D_MODEL,)).astype(dtype) @ w_out


def _hgrn2_mixer(h_lat, h_ctx, w_in, gnorm_w, w_out, lb, ctx_out):
    q_l, v_l, g_l, dirs_l = _hgrn2_project(h_lat, w_in, lb, True)
    q_c, v_c, g_c, dirs_c = _hgrn2_project(h_ctx, w_in, lb, ctx_out)
    s0 = jnp.zeros((h_lat.shape[0], HG_HEADS, HG_EXPAND, HG_DV), jnp.float32)
    o_l = 0.0
    o_c = 0.0
    for d, rev in enumerate((False, True)):
        (k_c, lf_c), (k_l, lf_l) = dirs_c[d], dirs_l[d]
        oc, s_ctx = _gla_chunk_scan(_flip(q_c, rev) if ctx_out else None, _flip(k_c, rev),
                                    _flip(v_c, rev), _flip(lf_c, rev), s0, ctx_out)
        ol, _ = _gla_chunk_scan(_flip(q_l, rev), _flip(k_l, rev), _flip(v_l, rev),
                                _flip(lf_l, rev), s_ctx, True)
        o_l = o_l + _flip(ol, rev)
        if ctx_out:
            o_c = o_c + _flip(oc, rev)
    y_l = _hgrn2_readout(o_l, g_l, gnorm_w, w_out, h_lat.dtype)
    y_c = _hgrn2_readout(o_c, g_c, gnorm_w, w_out, h_ctx.dtype) if ctx_out else None
    return y_l, y_c


def _shift_grid(h, rows):
    bsz, t, d = h.shape
    g = h.reshape(bsz, rows, GRID_W, d)
    c = d // 4
    left = jnp.pad(g[:, :, :-1, :c], ((0, 0), (0, 0), (1, 0), (0, 0)))
    right = jnp.pad(g[:, :, 1:, c:2 * c], ((0, 0), (0, 0), (0, 1), (0, 0)))
    up = jnp.pad(g[:, :-1, :, 2 * c:3 * c], ((0, 0), (1, 0), (0, 0), (0, 0)))
    down = jnp.pad(g[:, 1:, :, 3 * c:], ((0, 0), (0, 1), (0, 0), (0, 0)))
    return jnp.concatenate([left, right, up, down], axis=-1).reshape(bsz, t, d)


def _shift_seq(h):
    c = h.shape[-1] // 2
    prev = jnp.pad(h[:, :-1, :c], ((0, 0), (1, 0), (0, 0)))
    nxt = jnp.pad(h[:, 1:, c:], ((0, 0), (0, 1), (0, 0)))
    return jnp.concatenate([prev, nxt], axis=-1)


def _rwkv7_project(h, h_shift, mu, w_rkv, dec_w0, dec_w1, dec_w2, iclr_a0, iclr_a1, iclr_a2,
                   g1, g2, k_k, k_a, readout):
    f32 = jnp.float32

    def heads(a):
        return a.reshape(a.shape[:2] + (RW_HEADS, RW_HEAD)).astype(f32)

    dx = h_shift - h

    def lerp(j):
        return h + dx * mu[j]

    xw, xk, xv, xa = lerp(1), lerp(2), lerp(3), lerp(4)
    k = (xk @ w_rkv[1]).astype(f32)
    v = heads(xv @ w_rkv[2])
    kk = heads(k * k_k)
    kk = kk / jnp.maximum(jnp.sqrt(jnp.sum(kk * kk, axis=-1, keepdims=True)), 1e-12)
    dirs = []
    for d in range(2):
        w_log = -jax.nn.softplus(-(dec_w0[d] + jnp.tanh(xw @ dec_w1[d]) @ dec_w2[d]).astype(f32)) - 0.5
        decay = jnp.exp(-jnp.exp(w_log))
        a = jax.nn.sigmoid((iclr_a0[d] + (xa @ iclr_a1[d]) @ iclr_a2[d]).astype(f32))
        k_dir = k * (1 + (a - 1) * k_a)
        dirs.append((heads(decay), heads(a), heads(k_dir)))
    if readout:
        r = heads(lerp(0) @ w_rkv[0])
        g = jax.nn.sigmoid(lerp(5) @ g1) @ g2
    else:
        r, g = None, None
    return r, v, kk, g, dirs


def _rwkv7_scan(r, decay, k, v, kk, a, s0, with_output):
    def step(state, inp):
        w_t, k_t, v_t, kk_t, a_t = inp[0], inp[1], inp[2], inp[3], inp[4]
        sa = jnp.einsum('bhvk,bhk->bhv', state, kk_t)
        state = (state * w_t[:, :, None, :] - sa[..., None] * (kk_t * a_t)[:, :, None, :]
                 + v_t[..., None] * k_t[:, :, None, :])
        if not with_output:
            return state, None
        return state, jnp.einsum('bhvk,bhk->bhv', state, inp[5])

    seq = (decay, k, v, kk, a) + ((r,) if with_output else ())
    state, y = lax.scan(step, s0, tuple(jnp.moveaxis(t, 1, 0) for t in seq), unroll=4)
    if with_output:
        y = jnp.moveaxis(y, 0, 1)
    return y, state


def _rwkv7_readout(y, r, v, k_dirs, g, r_k, lnx_w, lnx_b, w_out, dtype):
    mean = jnp.mean(y, axis=-1, keepdims=True)
    var = jnp.mean(jnp.square(y - mean), axis=-1, keepdims=True)
    yn = (y - mean) * lax.rsqrt(var + RW_LNX_EPS)
    yn = yn * lnx_w.reshape(RW_HEADS, RW_HEAD) + lnx_b.reshape(RW_HEADS, RW_HEAD)
    for kd in k_dirs:
        yn = yn + jnp.sum(r * kd * r_k, axis=-1, keepdims=True) * v
    out = yn.reshape(y.shape[:2] + (D_MODEL,)).astype(dtype) * g
    return out @ w_out


def _rwkv7_mixer(h_lat, h_ctx, rows, mu, w_rkv, dec_w0, dec_w1, dec_w2, iclr_a0, iclr_a1, iclr_a2,
                 g1, g2, k_k, k_a, r_k, lnx_w, lnx_b, w_out, ctx_out):
    p = (mu, w_rkv, dec_w0, dec_w1, dec_w2, iclr_a0, iclr_a1, iclr_a2, g1, g2, k_k, k_a)
    r_l, v_l, kk_l, g_l, dirs_l = _rwkv7_project(h_lat, _shift_grid(h_lat, rows), *p, True)
    r_c, v_c, kk_c, g_c, dirs_c = _rwkv7_project(h_ctx, _shift_seq(h_ctx), *p, ctx_out)
    s0 = jnp.zeros((h_lat.shape[0], RW_HEADS, RW_HEAD, RW_HEAD), jnp.float32)
    y_l = 0.0
    y_c = 0.0
    for d, rev in enumerate((False, True)):
        (dec_c, a_c, k_c), (dec_l, a_l, k_l) = dirs_c[d], dirs_l[d]
        yc, s_ctx = _rwkv7_scan(_flip(r_c, rev) if ctx_out else None, _flip(dec_c, rev), _flip(k_c, rev),
                                _flip(v_c, rev), _flip(kk_c, rev), _flip(a_c, rev), s0, ctx_out)
        yl, _ = _rwkv7_scan(_flip(r_l, rev), _flip(dec_l, rev), _flip(k_l, rev), _flip(v_l, rev),
                            _flip(kk_l, rev), _flip(a_l, rev), s_ctx, True)
        y_l = y_l + _flip(yl, rev)
        if ctx_out:
            y_c = y_c + _flip(yc, rev)
    out_l = _rwkv7_readout(y_l, r_l, v_l, [dd[2] for dd in dirs_l], g_l, r_k, lnx_w, lnx_b, w_out, h_lat.dtype)
    out_c = (_rwkv7_readout(y_c, r_c, v_c, [dd[2] for dd in dirs_c], g_c, r_k, lnx_w, lnx_b, w_out, h_ctx.dtype)
             if ctx_out else None)
    return out_l, out_c


def _moe(tokens, router_w, router_b, w_gu, b_gu, w_down, b_down):
    n, d = tokens.shape
    logits = (tokens @ router_w + router_b).astype(jnp.float32)
    top_val, top_idx = lax.top_k(logits, TOP_K)
    gates = jax.nn.softmax(top_val, axis=-1).astype(tokens.dtype)
    flat_e = top_idx.reshape(-1)
    flat_tok = jnp.repeat(jnp.arange(n, dtype=jnp.int32), TOP_K)
    order = jnp.argsort(flat_e, stable=True)
    e_sorted = flat_e[order]
    counts = jnp.bincount(flat_e, length=N_EXPERTS)
    padded = (counts + MOE_BLOCK - 1) // MOE_BLOCK * MOE_BLOCK
    padded_end = jnp.cumsum(padded)
    padded_start = padded_end - padded
    start = jnp.cumsum(counts) - counts
    dest = padded_start[e_sorted] + jnp.arange(n * TOP_K, dtype=jnp.int32) - start[e_sorted]
    n_blocks = -(-(n * TOP_K + N_EXPERTS * (MOE_BLOCK - 1)) // MOE_BLOCK)
    cap = n_blocks * MOE_BLOCK
    slot_tok = jnp.full((cap,), n, jnp.int32).at[dest].set(flat_tok[order])
    slot_gate = jnp.zeros((cap,), tokens.dtype).at[dest].set(gates.reshape(-1)[order])
    block_e = jnp.minimum(jnp.searchsorted(padded_end, jnp.arange(n_blocks) * MOE_BLOCK, side='right'),
                          N_EXPERTS - 1)
    x_pad = jnp.concatenate([tokens, jnp.zeros((1, d), tokens.dtype)], axis=0)
    xb = x_pad[slot_tok].reshape(n_blocks, MOE_BLOCK, d)

    def expert_block(args):
        xblk, e = args
        gu = xblk @ w_gu[e] + b_gu[e]
        gate, up = jnp.split(gu, 2, axis=-1)
        gate = jnp.minimum(gate, SWIGLU_LIMIT)
        up = jnp.clip(up, -SWIGLU_LIMIT, SWIGLU_LIMIT)
        act = gate * jax.nn.sigmoid(SWIGLU_ALPHA * gate) * (up + 1)
        return act @ w_down[e] + b_down[e]

    yb = lax.map(expert_block, (xb, block_e))
    y = yb.reshape(cap, d) * slot_gate[:, None]
    return jnp.zeros((n + 1, d), tokens.dtype).at[slot_tok].add(y)[:n]


def setup_inputs(seed: int = 0) -> dict:
    key = jax.random.key(seed)
    ks = iter(jax.random.split(key, 64))
    D = D_MODEL
    n_hg = (DEPTH + 1) // 2
    n_rw = DEPTH // 2

    def nrm(shape, s):
        return jax.random.normal(next(ks), shape, jnp.float32) * s

    def uni(shape, lo, hi):
        return jax.random.uniform(next(ks), shape, jnp.float32, lo, hi)

    return {
        'x': nrm((BATCH, SEQ, D), 1.0),
        'c': nrm((BATCH, D), 1.0),
        'ctx': nrm((BATCH, CTX_LEN, D), 1.0),
        'c_ctx': nrm((D,), 1.0),
        'ada_w': nrm((DEPTH, D, 6 * D), 0.5 * D ** -0.5),
        'ada_b': nrm((DEPTH, 6 * D), 0.02),
        'norm_mix_g': 1.0 + nrm((DEPTH, D), 0.1),
        'norm_ffn_g': 1.0 + nrm((DEPTH, D), 0.1),
        'hg_w_in': nrm((n_hg, D, 3 * HG_FK + 2 * D), D ** -0.5),
        'hg_gnorm_w': 1.0 + nrm((n_hg, HG_DV), 0.1),
        'hg_w_out': nrm((n_hg, D, D), D ** -0.5),
        'hg_lb': nrm((2, DEPTH + 1, HG_FK), 0.5),
        'rw_mu': uni((n_rw, 6, D), 0.0, 1.0),
        'rw_w_rkv': nrm((n_rw, 3, D, D), D ** -0.5),
        'rw_dec_w0': uni((n_rw, 2, D), -3.5, -0.5),
        'rw_dec_w1': nrm((n_rw, 2, D, RW_DECAY_LORA), D ** -0.5),
        'rw_dec_w2': nrm((n_rw, 2, RW_DECAY_LORA, D), 0.1 * RW_DECAY_LORA ** -0.5),
        'rw_iclr_a0': nrm((n_rw, 2, D), 0.5),
        'rw_iclr_a1': nrm((n_rw, 2, D, RW_ICLR_LORA), D ** -0.5),
        'rw_iclr_a2': nrm((n_rw, 2, RW_ICLR_LORA, D), 0.3 * RW_ICLR_LORA ** -0.5),
        'rw_g1': nrm((n_rw, D, RW_GATE_LORA), D ** -0.5),
        'rw_g2': nrm((n_rw, RW_GATE_LORA, D), RW_GATE_LORA ** -0.5),
        'rw_k_k': 0.85 + nrm((n_rw, D), 0.05),
        'rw_k_a': 1.0 + nrm((n_rw, D), 0.1),
        'rw_r_k': nrm((n_rw, RW_HEADS, RW_HEAD), 0.1),
        'rw_lnx_w': 1.0 + nrm((n_rw, D), 0.1),
        'rw_lnx_b': nrm((n_rw, D), 0.02),
        'rw_w_out': nrm((n_rw, D, D), D ** -0.5),
        'moe_router_w': nrm((DEPTH, D, N_EXPERTS), D ** -0.5),
        'moe_router_b': nrm((DEPTH, N_EXPERTS), 0.01),
        'moe_w_gu': nrm((DEPTH, N_EXPERTS, D, 2 * D_EXPERT), D ** -0.5),
        'moe_b_gu': nrm((DEPTH, N_EXPERTS, 2 * D_EXPERT), 0.02),
        'moe_w_down': nrm((DEPTH, N_EXPERTS, D_EXPERT, D), D_EXPERT ** -0.5),
        'moe_b_down': nrm((DEPTH, N_EXPERTS, D), 0.02),
        'final_g': 1.0 + nrm((D,), 0.1),
    }


def reference(x, c, ctx, c_ctx, ada_w, ada_b, norm_mix_g, norm_ffn_g,
              hg_w_in, hg_gnorm_w, hg_w_out, hg_lb,
              rw_mu, rw_w_rkv, rw_dec_w0, rw_dec_w1, rw_dec_w2, rw_iclr_a0, rw_iclr_a1, rw_iclr_a2,
              rw_g1, rw_g2, rw_k_k, rw_k_a, rw_r_k, rw_lnx_w, rw_lnx_b, rw_w_out,
              moe_router_w, moe_router_b, moe_w_gu, moe_b_gu, moe_w_down, moe_b_down, final_g):
    bsz, n_lat, d = x.shape
    rows = n_lat // GRID_W
    lb_all = jnp.cumsum(jax.nn.softmax(hg_lb.astype(jnp.float32), axis=1), axis=1)
    s_c = jax.nn.silu(c)
    s_cc = jax.nn.silu(c_ctx)
    xl, xc = x, ctx
    for layer in range(DEPTH):
        last = layer == DEPTH - 1
        mod_l = jnp.split((s_c @ ada_w[layer] + ada_b[layer])[:, None, :], 6, axis=-1)
        mod_c = jnp.split(s_cc @ ada_w[layer] + ada_b[layer], 6, axis=-1)
        hl = _modulate(_rmsnorm(xl, norm_mix_g[layer]), mod_l[0], mod_l[1])
        hc = _modulate(_rmsnorm(xc, norm_mix_g[layer]), mod_c[0], mod_c[1])
        j = layer // N_MIXERS
        if layer % N_MIXERS == 0:
            yl, yc = _hgrn2_mixer(hl, hc, hg_w_in[j], hg_gnorm_w[j], hg_w_out[j],
                                  lb_all[:, layer], not last)
        else:
            yl, yc = _rwkv7_mixer(hl, hc, rows, rw_mu[j], rw_w_rkv[j], rw_dec_w0[j], rw_dec_w1[j],
                                  rw_dec_w2[j], rw_iclr_a0[j], rw_iclr_a1[j], rw_iclr_a2[j],
                                  rw_g1[j], rw_g2[j], rw_k_k[j], rw_k_a[j], rw_r_k[j],
                                  rw_lnx_w[j], rw_lnx_b[j], rw_w_out[j], not last)
        xl = xl + mod_l[2] * yl
        hl = _modulate(_rmsnorm(xl, norm_ffn_g[layer]), mod_l[3], mod_l[4])
        moe_p = (moe_router_w[layer], moe_router_b[layer], moe_w_gu[layer], moe_b_gu[layer],
                 moe_w_down[layer], moe_b_down[layer])
        if last:
            xl = xl + mod_l[5] * _moe(hl.reshape(-1, d), *moe_p).reshape(hl.shape)
        else:
            xc = xc + mod_c[2] * yc
            hc = _modulate(_rmsnorm(xc, norm_ffn_g[layer]), mod_c[3], mod_c[4])
            n_l = bsz * n_lat
            out = _moe(jnp.concatenate([hl.reshape(-1, d), hc.reshape(-1, d)], axis=0), *moe_p)
            xl = xl + mod_l[5] * out[:n_l].reshape(hl.shape)
            xc = xc + mod_c[5] * out[n_l:].reshape(hc.shape)
    return _rmsnorm(xl, final_g)
```

```python
import functools

import jax
import jax.numpy as jnp
from jax import lax
from jax.experimental import pallas as pl
from jax.experimental.pallas import tpu as pltpu

F32 = jnp.float32
BF16 = jnp.bfloat16
HIGHEST = lax.Precision.HIGHEST

GRID_W = 64
HG_HEADS = 8
HG_DK = 128
RW_HEADS = 16
RW_DK = 64
N_EXPERTS = 32
TOP_K = 4
SWIGLU_LIMIT = 7.0
SWIGLU_ALPHA = 1.702
NORM_EPS = 1e-6
RW_LNX_EPS = 1e-5 * RW_DK

CHUNK = 64
SUB = 16
SEQ_BLOCK = 256
MOE_BLOCK = 256
VMEM_LIMIT = 56 * 1024 * 1024

_NT = (((1,), (1,)), ((), ()))


def _dot(a, b):
    return jnp.dot(a.astype(BF16), b.astype(BF16), preferred_element_type=F32)


def _dot_nt(a, b):
    return lax.dot_general(a.astype(BF16), b.astype(BF16), _NT, preferred_element_type=F32)


def _split(a):
    hi = a.astype(BF16)
    lo = (a - hi.astype(F32)).astype(BF16)
    return hi, lo


def _dot3(a, b, nt=False):
    a_hi, a_lo = _split(a)
    b_hi, b_lo = _split(b)
    if nt:
        f = lambda x, y: lax.dot_general(x, y, _NT, preferred_element_type=F32)
    else:
        f = lambda x, y: jnp.dot(x, y, preferred_element_type=F32)
    return f(a_hi, b_hi) + (f(a_hi, b_lo) + f(a_lo, b_hi))


def _mm_body(x_ref, w_ref, b_ref, o_ref, wb_ref):
    @pl.when(pl.program_id(1) == 0)
    def _():
        wb_ref[...] = w_ref[...].astype(BF16)

    acc = jnp.dot(x_ref[...].astype(BF16), wb_ref[...], preferred_element_type=F32)
    o_ref[...] = (acc + b_ref[...]).astype(o_ref.dtype)


def _mm_exact_body(x_ref, w_ref, b_ref, o_ref):
    acc = jnp.dot(x_ref[...], w_ref[...], precision=HIGHEST, preferred_element_type=F32)
    o_ref[...] = acc + b_ref[...]


def _mm(x, w, b=None, out_dtype=F32, tm=512, tn=512):
    m, k = x.shape
    n = w.shape[1]
    tm = min(tm, m)
    tn = min(tn, n)
    assert m % tm == 0 and n % tn == 0, (m, n, tm, tn)
    if b is None:
        b = jnp.zeros((n,), F32)
    return pl.pallas_call(
        _mm_body,
        grid=(n // tn, m // tm),
        in_specs=[
            pl.BlockSpec((tm, k), lambda j, i: (i, 0)),
            pl.BlockSpec((k, tn), lambda j, i: (0, j)),
            pl.BlockSpec((1, tn), lambda j, i: (0, j)),
        ],
        out_specs=pl.BlockSpec((tm, tn), lambda j, i: (i, j)),
        out_shape=jax.ShapeDtypeStruct((m, n), out_dtype),
        scratch_shapes=[pltpu.VMEM((k, tn), BF16)],
        compiler_params=pltpu.CompilerParams(
            dimension_semantics=("arbitrary", "arbitrary"), vmem_limit_bytes=VMEM_LIMIT),
        name="mm",
    )(x, w, b.reshape(1, n).astype(F32))


def _mm_exact(x, w, b, tn=512):
    m, k = x.shape
    n = w.shape[1]
    tn = min(tn, n)
    return pl.pallas_call(
        _mm_exact_body,
        grid=(n // tn,),
        in_specs=[
            pl.BlockSpec((m, k), lambda j: (0, 0)),
            pl.BlockSpec((k, tn), lambda j: (0, j)),
            pl.BlockSpec((1, tn), lambda j: (0, j)),
        ],
        out_specs=pl.BlockSpec((m, tn), lambda j: (0, j)),
        out_shape=jax.ShapeDtypeStruct((m, n), F32),
        compiler_params=pltpu.CompilerParams(
            dimension_semantics=("arbitrary",), vmem_limit_bytes=VMEM_LIMIT),
        name="mm_exact",
    )(x, w, b.reshape(1, n))


def _seq_block_index(c, n_blocks, n_ctx_blocks, rev):
    if not rev:
        return c
    return jnp.where(c < n_ctx_blocks, n_ctx_blocks - 1 - c, n_blocks - 1 - (c - n_ctx_blocks))


def _gla_chunk(q, f, v, lb, st, rev, consts):
    tri, ones_kk, trow = consts
    q = q * jax.nn.sigmoid(q)
    g = jnp.log(lb + (1.0 - lb) * jax.nn.sigmoid(f))
    k = (1.0 - lb) * jax.nn.sigmoid(-f)
    b = jnp.dot(tri, g, precision=HIGHEST, preferred_element_type=F32)
    b_end = b[0:1] if rev else b[CHUNK - 1:CHUNK]
    o_inter = _dot_nt(q * jnp.exp(b), st)
    st_new = st * jnp.exp(b_end) + _dot(v.T, k * jnp.exp(b_end - b))

    outs = [None] * (CHUNK // SUB)
    for p in range(CHUNK // SUB):
        if rev:
            r0 = CHUNK - SUB * (p + 1)
            e0, e1, mrow = CHUNK - SUB * p, CHUNK, CHUNK - SUB * p
        else:
            r0 = SUB * p
            e0, e1, mrow = 0, SUB * p, SUB * p - 1
        qp, bp, kp, vp = (x[r0:r0 + SUB] for x in (q, b, k, v))
        acc = o_inter[r0:r0 + SUB]
        if p > 0:
            m = b[mrow:mrow + 1]
            sc = _dot_nt(qp * jnp.exp(bp - m), k[e0:e1] * jnp.exp(m - b[e0:e1]))
            acc = acc + _dot(sc, v[e0:e1])
        terms = []
        for s in range(SUB):
            mask = (trow <= s) if rev else (trow >= s)
            w = jnp.exp(jnp.where(mask, bp - bp[s:s + 1], -1e30))
            terms.append(qp * kp[s:s + 1] * w)
        sums = _dot(jnp.concatenate(terms, axis=0), ones_kk)
        for s in range(SUB):
            acc = acc + sums[s * SUB:(s + 1) * SUB] * vp[s:s + 1]
        outs[r0 // SUB] = acc
    return jnp.concatenate(outs, axis=0), st_new


def _gla_body(q_ref, f_ref, v_ref, lb_ref, o_ref, st_ref, *, rev):
    @pl.when(pl.program_id(2) == 0)
    def _():
        st_ref[...] = jnp.zeros_like(st_ref)

    row = lax.broadcasted_iota(jnp.int32, (CHUNK, CHUNK), 0)
    col = lax.broadcasted_iota(jnp.int32, (CHUNK, CHUNK), 1)
    tri = ((row <= col) if rev else (row >= col)).astype(F32)
    ones_kk = jnp.ones((HG_DK, HG_DK), BF16)
    trow = lax.broadcasted_iota(jnp.int32, (SUB, HG_DK), 0)
    lb = lb_ref[0]
    n_chunks = SEQ_BLOCK // CHUNK

    def step(i, carry):
        cc = (n_chunks - 1 - i) if rev else i
        sl = pl.ds(pl.multiple_of(cc * CHUNK, CHUNK), CHUNK)
        o, st = _gla_chunk(q_ref[0, sl, :], f_ref[0, sl, :], v_ref[0, sl, :], lb,
                           st_ref[...], rev, (tri, ones_kk, trow))
        o_ref[0, sl, :] = o
        st_ref[...] = st
        return carry

    lax.fori_loop(0, n_chunks, step, 0)


def _gla_scan(z, lb, n_ctx, rev):
    bsz, t, _ = z.shape
    assert t % SEQ_BLOCK == 0 and n_ctx % SEQ_BLOCK == 0
    nb = t // SEQ_BLOCK
    d_model = HG_HEADS * HG_DK
    f_off = HG_HEADS * (2 if rev else 1)
    v_off = HG_HEADS * 3
    tmap = lambda c: _seq_block_index(c, nb, n_ctx // SEQ_BLOCK, rev)
    return pl.pallas_call(
        functools.partial(_gla_body, rev=rev),
        grid=(bsz, HG_HEADS, nb),
        in_specs=[
            pl.BlockSpec((1, SEQ_BLOCK, HG_DK), lambda b, h, c: (b, tmap(c), h)),
            pl.BlockSpec((1, SEQ_BLOCK, HG_DK), lambda b, h, c: (b, tmap(c), f_off + h)),
            pl.BlockSpec((1, SEQ_BLOCK, HG_DK), lambda b, h, c: (b, tmap(c), v_off + h)),
            pl.BlockSpec((1, 1, HG_DK), lambda b, h, c: (h, 0, 0)),
        ],
        out_specs=pl.BlockSpec((1, SEQ_BLOCK, HG_DK), lambda b, h, c: (b, tmap(c), h)),
        out_shape=jax.ShapeDtypeStruct((bsz, t, d_model), F32),
        scratch_shapes=[pltpu.VMEM((HG_DK, HG_DK), F32)],
        compiler_params=pltpu.CompilerParams(
            dimension_semantics=("arbitrary", "arbitrary", "arbitrary"),
            vmem_limit_bytes=VMEM_LIMIT),
        name="gla_rev" if rev else "gla_fwd",
    )(z, z, z, lb.reshape(HG_HEADS, 1, HG_DK))


RW_HB = 4


def _rwkv_chunk_head(lw, cum, r, k, v, kap, a, st, rev, consts):
    incl, strict, eye = consts
    cum_ex = cum - lw
    tot = cum[0:1] if rev else cum[CHUNK - 1:CHUNK]
    e_neg = jnp.exp(-cum)
    kt = kap * jnp.exp(cum_ex)
    bb = kap * a
    bt = bb * e_neg
    kd = k * e_neg
    rt = r * jnp.exp(cum)
    dec_end = jnp.exp(tot - cum)
    b_end = bb * dec_end
    k_end = k * dec_end

    gram = _dot_nt(jnp.concatenate([kt, rt], axis=0), jnp.concatenate([bt, kd], axis=0))
    a_bk = jnp.where(strict, gram[:CHUNK, :CHUNK], 0.0)
    a_kk = jnp.where(strict, gram[:CHUNK, CHUNK:], 0.0)
    a_rb = jnp.where(incl, gram[CHUNK:, :CHUNK], 0.0)
    a_rk = jnp.where(incl, gram[CHUNK:, CHUNK:], 0.0)

    tinv = eye - a_bk
    pw = _dot3(a_bk, a_bk)
    for _ in range(4):
        tp = _dot3(jnp.concatenate([tinv, pw], axis=0), pw)
        tinv = tinv + tp[:CHUNK]
        pw = tp[CHUNK:]
    tinv = tinv + _dot3(tinv, pw)

    av = _dot(jnp.concatenate([a_kk, a_rk], axis=0), v)
    ku = _dot3(tinv, jnp.concatenate([kt, av[:CHUNK]], axis=1))
    ab = _dot(a_rb, ku)
    rh = rt - ab[:, :RW_DK]
    y0 = av[CHUNK:] - ab[:, RW_DK:]
    mb = _dot(ku.T, b_end)
    m_mat = jnp.where(eye > 0, jnp.exp(tot), 0.0) - mb[:RW_DK]
    n_mat = _dot(v.T, k_end) - mb[RW_DK:]

    y = _dot3(rh, st, nt=True) + y0
    st_new = _dot3(st, m_mat) + n_mat
    return y, st_new


def _rwkv_body(lw_ref, r_ref, k_ref, v_ref, kap_ref, a_ref, y_ref, st_ref, *, rev):
    @pl.when(pl.program_id(2) == 0)
    def _():
        st_ref[...] = jnp.zeros_like(st_ref)

    row = lax.broadcasted_iota(jnp.int32, (CHUNK, CHUNK), 0)
    col = lax.broadcasted_iota(jnp.int32, (CHUNK, CHUNK), 1)
    incl = (row <= col) if rev else (row >= col)
    strict = (row < col) if rev else (row > col)
    eye = (row == col).astype(F32)
    tri = incl.astype(F32)
    n_chunks = SEQ_BLOCK // CHUNK

    def step(i, carry):
        cc = (n_chunks - 1 - i) if rev else i
        sl = pl.ds(pl.multiple_of(cc * CHUNK, CHUNK), CHUNK)
        lw = lw_ref[0, sl, :]
        cum = jnp.dot(tri, lw, precision=HIGHEST, preferred_element_type=F32)
        r, k, v, kap, a = (ref[0, sl, :] for ref in (r_ref, k_ref, v_ref, kap_ref, a_ref))
        ys = []
        for h in range(RW_HB):
            hs = slice(h * RW_DK, (h + 1) * RW_DK)
            y, st = _rwkv_chunk_head(lw[:, hs], cum[:, hs], r[:, hs], k[:, hs], v[:, hs],
                                     kap[:, hs], a[:, hs], st_ref[h], rev, (incl, strict, eye))
            st_ref[h] = st
            ys.append(y)
        y_ref[0, sl, :] = jnp.concatenate(ys, axis=1)
        return carry

    lax.fori_loop(0, n_chunks, step, 0)


def _rwkv_scan(lw, r, k, v, kap, a, n_ctx, rev):
    bsz, t, d_model = lw.shape
    assert t % SEQ_BLOCK == 0 and n_ctx % SEQ_BLOCK == 0
    nb = t // SEQ_BLOCK
    width = RW_HB * RW_DK
    tmap = lambda c: _seq_block_index(c, nb, n_ctx // SEQ_BLOCK, rev)
    spec = pl.BlockSpec((1, SEQ_BLOCK, width), lambda b, h, c: (b, tmap(c), h))
    return pl.pallas_call(
        functools.partial(_rwkv_body, rev=rev),
        grid=(bsz, d_model // width, nb),
        in_specs=[spec] * 6,
        out_specs=spec,
        out_shape=jax.ShapeDtypeStruct((bsz, t, d_model), F32),
        scratch_shapes=[pltpu.VMEM((RW_HB, RW_DK, RW_DK), F32)],
        compiler_params=pltpu.CompilerParams(
            dimension_semantics=("arbitrary", "arbitrary", "arbitrary"),
            vmem_limit_bytes=VMEM_LIMIT),
        name="rwkv_rev" if rev else "rwkv_fwd",
    )(lw, r, k, v, kap, a)


def _router_body(x_ref, w_ref, b_ref, idx_ref, gate_ref):
    logits = jnp.dot(x_ref[...], w_ref[...], precision=HIGHEST,
                     preferred_element_type=F32) + b_ref[...]
    lane = lax.broadcasted_iota(jnp.int32, logits.shape, 1)
    vals, idxs = [], []
    for _ in range(TOP_K):
        m = jnp.max(logits, axis=-1, keepdims=True)
        idx = jnp.min(jnp.where(logits == m, lane, N_EXPERTS), axis=-1, keepdims=True)
        vals.append(m)
        idxs.append(idx)
        logits = jnp.where(lane == idx, -jnp.inf, logits)
    exps = [jnp.exp(val - vals[0]) for val in vals]
    denom = exps[0] + exps[1] + exps[2] + exps[3]
    idx_ref[...] = jnp.concatenate(idxs, axis=1)
    gate_ref[...] = jnp.concatenate(exps, axis=1) / denom


def _router(x, w, b, tm=512):
    n, d = x.shape
    return pl.pallas_call(
        _router_body,
        grid=(n // tm,),
        in_specs=[
            pl.BlockSpec((tm, d), lambda i: (i, 0)),
            pl.BlockSpec((d, N_EXPERTS), lambda i: (0, 0)),
            pl.BlockSpec((1, N_EXPERTS), lambda i: (0, 0)),
        ],
        out_specs=[pl.BlockSpec((tm, TOP_K), lambda i: (i, 0)),
                   pl.BlockSpec((tm, TOP_K), lambda i: (i, 0))],
        out_shape=[jax.ShapeDtypeStruct((n, TOP_K), jnp.int32),
                   jax.ShapeDtypeStruct((n, TOP_K), F32)],
        compiler_params=pltpu.CompilerParams(
            dimension_semantics=("arbitrary",), vmem_limit_bytes=VMEM_LIMIT),
        name="router",
    )(x, w, b.reshape(1, N_EXPERTS))


EXPERT_COLS = 512


def _expert_body(be_ref, nu_ref, x_ref, wgu_ref, bgu_ref, wd_ref, bd_ref, g_ref, o_ref,
                 wgu_b, wd_b):
    i = pl.program_id(0)
    d_exp = wd_ref.shape[1]
    prev = be_ref[jnp.maximum(i - 1, 0)]

    @pl.when((i == 0) | (be_ref[i] != prev))
    def _():
        wgu_b[...] = wgu_ref[0].astype(BF16)
        wd_b[...] = wd_ref[0].astype(BF16)

    @pl.when(i < nu_ref[0])
    def _():
        x = x_ref[...]
        acc = jnp.zeros(o_ref.shape, F32)
        for c0 in range(0, d_exp, EXPERT_COLS):
            gate = jnp.dot(x, wgu_b[:, c0:c0 + EXPERT_COLS], preferred_element_type=F32)
            gate = gate + bgu_ref[0, :, c0:c0 + EXPERT_COLS]
            up = jnp.dot(x, wgu_b[:, d_exp + c0:d_exp + c0 + EXPERT_COLS],
                         preferred_element_type=F32)
            up = up + bgu_ref[0, :, d_exp + c0:d_exp + c0 + EXPERT_COLS]
            gate = jnp.minimum(gate, SWIGLU_LIMIT)
            up = jnp.clip(up, -SWIGLU_LIMIT, SWIGLU_LIMIT)
            act = gate * jax.nn.sigmoid(SWIGLU_ALPHA * gate) * (up + 1.0)
            acc = acc + jnp.dot(act.astype(BF16), wd_b[c0:c0 + EXPERT_COLS, :],
                                preferred_element_type=F32)
        o_ref[...] = (acc + bd_ref[0]) * g_ref[...]

    @pl.when(i >= nu_ref[0])
    def _():
        o_ref[...] = jnp.zeros_like(o_ref)


def _experts(xs, slot_gate, block_e, n_used, w_gu, b_gu, w_down, b_down):
    cap, d = xs.shape
    n_e, _, d2 = w_gu.shape
    d_exp = d2 // 2
    nb = cap // MOE_BLOCK
    grid_spec = pltpu.PrefetchScalarGridSpec(
        num_scalar_prefetch=2,
        grid=(nb,),
        in_specs=[
            pl.BlockSpec((MOE_BLOCK, d), lambda i, be, nu: (i, 0)),
            pl.BlockSpec((1, d, d2), lambda i, be, nu: (be[i], 0, 0)),
            pl.BlockSpec((1, 1, d2), lambda i, be, nu: (be[i], 0, 0)),
            pl.BlockSpec((1, d_exp, d), lambda i, be, nu: (be[i], 0, 0)),
            pl.BlockSpec((1, 1, d), lambda i, be, nu: (be[i], 0, 0)),
            pl.BlockSpec((MOE_BLOCK, 1), lambda i, be, nu: (i, 0)),
        ],
        out_specs=pl.BlockSpec((MOE_BLOCK, d), lambda i, be, nu: (i, 0)),
        scratch_shapes=[pltpu.VMEM((d, d2), BF16), pltpu.VMEM((d_exp, d), BF16)],
    )
    return pl.pallas_call(
        _expert_body,
        grid_spec=grid_spec,
        out_shape=jax.ShapeDtypeStruct((cap, d), F32),
        compiler_params=pltpu.CompilerParams(
            dimension_semantics=("arbitrary",), vmem_limit_bytes=VMEM_LIMIT),
        name="experts",
    )(block_e, n_used, xs, w_gu, b_gu.reshape(n_e, 1, d2), w_down, b_down.reshape(n_e, 1, d),
      slot_gate.reshape(cap, 1))


def _moe(tokens, router_w, router_b, w_gu, b_gu, w_down, b_down):
    n, d = tokens.shape
    top_idx, gates = _router(tokens, router_w, router_b)
    flat_e = top_idx.reshape(-1)
    onehot = (flat_e[:, None] == jnp.arange(N_EXPERTS, dtype=jnp.int32)[None, :]).astype(jnp.int32)
    rank = jnp.sum((jnp.cumsum(onehot, axis=0) - 1) * onehot, axis=1)
    counts = jnp.sum(onehot, axis=0)
    padded = (counts + MOE_BLOCK - 1) // MOE_BLOCK * MOE_BLOCK
    padded_end = jnp.cumsum(padded)
    padded_start = padded_end - padded
    dest = padded_start[flat_e] + rank
    n_blocks = -(-(n * TOP_K + N_EXPERTS * (MOE_BLOCK - 1)) // MOE_BLOCK)
    cap = n_blocks * MOE_BLOCK
    flat_tok = jnp.repeat(jnp.arange(n, dtype=jnp.int32), TOP_K)
    slot_tok = jnp.full((cap,), n, jnp.int32).at[dest].set(flat_tok)
    slot_gate = jnp.zeros((cap,), F32).at[dest].set(gates.reshape(-1))
    n_used = (padded_end[-1] // MOE_BLOCK).astype(jnp.int32)
    blk = jnp.minimum(jnp.arange(n_blocks, dtype=jnp.int32), n_used - 1) * MOE_BLOCK
    block_e = jnp.minimum(jnp.searchsorted(padded_end, blk, side='right'),
                          N_EXPERTS - 1).astype(jnp.int32)
    x_pad = jnp.concatenate([tokens.astype(BF16), jnp.zeros((1, d), BF16)], axis=0)
    xs = x_pad[slot_tok]
    ys = _experts(xs, slot_gate, block_e, n_used.reshape(1), w_gu, b_gu, w_down, b_down)
    return jnp.sum(ys[dest].reshape(n, TOP_K, d), axis=1)


def _rmsnorm(x, g):
    return x * lax.rsqrt(jnp.mean(x * x, axis=-1, keepdims=True) + NORM_EPS) * g


def _mm_tokens(h, w, b=None, **kw):
    lead = h.shape[:-1]
    return _mm(h.reshape(-1, h.shape[-1]), w, b, **kw).reshape(lead + (w.shape[1],))


def _shift_grid(h):
    bsz, t, d = h.shape
    g = h.reshape(bsz, t // GRID_W, GRID_W, d)
    c = d // 4
    left = jnp.pad(g[:, :, :-1, :c], ((0, 0), (0, 0), (1, 0), (0, 0)))
    right = jnp.pad(g[:, :, 1:, c:2 * c], ((0, 0), (0, 0), (0, 1), (0, 0)))
    up = jnp.pad(g[:, :-1, :, 2 * c:3 * c], ((0, 0), (1, 0), (0, 0), (0, 0)))
    down = jnp.pad(g[:, 1:, :, 3 * c:], ((0, 0), (0, 1), (0, 0), (0, 0)))
    return jnp.concatenate([left, right, up, down], axis=-1).reshape(bsz, t, d)


def _shift_seq(h):
    c = h.shape[-1] // 2
    prev = jnp.pad(h[:, :-1, :c], ((0, 0), (1, 0), (0, 0)))
    nxt = jnp.pad(h[:, 1:, c:], ((0, 0), (0, 1), (0, 0)))
    return jnp.concatenate([prev, nxt], axis=-1)


def _hgrn2_mixer(h, n_ctx, w_in, gnorm_w, w_out, lb):
    bsz, t, d = h.shape
    z = _mm_tokens(h.astype(BF16), w_in)
    o = _gla_scan(z, lb[0], n_ctx, False) + _gla_scan(z, lb[1], n_ctx, True)
    g = z[..., 4 * d:]
    oh = o.reshape(bsz, t, HG_HEADS, HG_DK)
    oh = _rmsnorm(oh, gnorm_w) * jax.nn.silu(g.reshape(bsz, t, HG_HEADS, HG_DK))
    return _mm_tokens(oh.reshape(bsz, t, d).astype(BF16), w_out)


def _rwkv7_mixer(h, n_ctx, mu, w_rkv, dec_w0, dec_w1, dec_w2, iclr_a0, iclr_a1, iclr_a2,
                 g1, g2, k_k, k_a, r_k, lnx_w, lnx_b, w_out):
    bsz, t, d = h.shape
    h_shift = jnp.concatenate([_shift_seq(h[:, :n_ctx]), _shift_grid(h[:, n_ctx:])], axis=1)
    dx = h_shift - h
    xr, xw, xk, xv, xa, xg = ((h + dx * mu[j]).astype(BF16) for j in range(6))
    heads = lambda a: a.reshape(bsz, t, RW_HEADS, RW_DK)
    r = _mm_tokens(xr, w_rkv[0])
    k = _mm_tokens(xk, w_rkv[1])
    v = _mm_tokens(xv, w_rkv[2])
    kk = heads(k * k_k)
    kk = (kk / jnp.maximum(jnp.sqrt(jnp.sum(kk * kk, axis=-1, keepdims=True)), 1e-12)).reshape(bsz, t, d)
    lora_w = jnp.tanh(_mm_tokens(xw, jnp.concatenate([dec_w1[0], dec_w1[1]], axis=1)))
    lora_a = _mm_tokens(xa, jnp.concatenate([iclr_a1[0], iclr_a1[1]], axis=1))
    nw = dec_w1.shape[-1]
    na = iclr_a1.shape[-1]
    y = 0.0
    bonus = 0.0
    for dd, rev in enumerate((False, True)):
        zw = _mm_tokens(lora_w[..., dd * nw:(dd + 1) * nw], dec_w2[dd], dec_w0[dd])
        lw = -jnp.exp(-jax.nn.softplus(-zw) - 0.5)
        a = jax.nn.sigmoid(_mm_tokens(lora_a[..., dd * na:(dd + 1) * na], iclr_a2[dd], iclr_a0[dd]))
        k_dir = k * (1.0 + (a - 1.0) * k_a)
        y = y + _rwkv_scan(lw, r, k_dir, v, kk, a, n_ctx, rev)
        bonus = bonus + jnp.sum(heads(r * k_dir) * r_k, axis=-1, keepdims=True)
    gate = _mm_tokens(jax.nn.sigmoid(_mm_tokens(xg, g1)), g2)
    yh = heads(y)
    mean = jnp.mean(yh, axis=-1, keepdims=True)
    var = jnp.mean(jnp.square(yh - mean), axis=-1, keepdims=True)
    yn = (yh - mean) * lax.rsqrt(var + RW_LNX_EPS)
    yn = yn * lnx_w.reshape(RW_HEADS, RW_DK) + lnx_b.reshape(RW_HEADS, RW_DK)
    yn = yn + bonus * heads(v)
    return _mm_tokens((yn.reshape(bsz, t, d) * gate).astype(BF16), w_out)


def kernel(x, c, ctx, c_ctx, ada_w, ada_b, norm_mix_g, norm_ffn_g, hg_w_in, hg_gnorm_w, hg_w_out, hg_lb, rw_mu, rw_w_rkv, rw_dec_w0, rw_dec_w1, rw_dec_w2, rw_iclr_a0, rw_iclr_a1, rw_iclr_a2, rw_g1, rw_g2, rw_k_k, rw_k_a, rw_r_k, rw_lnx_w, rw_lnx_b, rw_w_out, moe_router_w, moe_router_b, moe_w_gu, moe_b_gu, moe_w_down, moe_b_down, final_g):
    bsz, n_lat, d = x.shape
    n_ctx = ctx.shape[1]
    depth = ada_w.shape[0]
    lb_all = jnp.cumsum(jax.nn.softmax(hg_lb.astype(F32), axis=1), axis=1)
    cond = jnp.concatenate([jax.nn.silu(c), jax.nn.silu(c_ctx)[None, :],
                            jnp.zeros((8 - bsz - 1, d), F32)], axis=0)
    xs = jnp.concatenate([ctx, x], axis=1)
    t = n_ctx + n_lat
    for layer in range(depth):
        mod = _mm_exact(cond, ada_w[layer], ada_b[layer])
        mod_l = mod[:bsz].reshape(bsz, 1, 6, d)
        mod_c = jnp.broadcast_to(mod[bsz].reshape(1, 1, 6, d), (bsz, 1, 6, d))
        seg = lambda j: jnp.concatenate(
            [jnp.broadcast_to(mod_c[:, :, j], (bsz, n_ctx, d)),
             jnp.broadcast_to(mod_l[:, :, j], (bsz, n_lat, d))], axis=1)
        h = _rmsnorm(xs, norm_mix_g[layer]) * (1.0 + seg(1)) + seg(0)
        j = layer // 2
        if layer % 2 == 0:
            y = _hgrn2_mixer(h, n_ctx, hg_w_in[j], hg_gnorm_w[j], hg_w_out[j], lb_all[:, layer])
        else:
            y = _rwkv7_mixer(h, n_ctx, rw_mu[j], rw_w_rkv[j], rw_dec_w0[j], rw_dec_w1[j],
                             rw_dec_w2[j], rw_iclr_a0[j], rw_iclr_a1[j], rw_iclr_a2[j],
                             rw_g1[j], rw_g2[j], rw_k_k[j], rw_k_a[j], rw_r_k[j],
                             rw_lnx_w[j], rw_lnx_b[j], rw_w_out[j])
        xs = xs + seg(2) * y
        h = _rmsnorm(xs, norm_ffn_g[layer]) * (1.0 + seg(4)) + seg(3)
        out = _moe(h.reshape(bsz * t, d), moe_router_w[layer], moe_router_b[layer],
                   moe_w_gu[layer], moe_b_gu[layer], moe_w_down[layer], moe_b_down[layer])
        xs = xs + seg(5) * out.reshape(bsz, t, d)
    return _rmsnorm(xs[:, n_ctx:], final_g)
```

```python
import functools

import jax
import jax.numpy as jnp
from jax import lax
from jax.experimental import pallas as pl
from jax.experimental.pallas import tpu as pltpu

F32 = jnp.float32
BF16 = jnp.bfloat16
HIGHEST = lax.Precision.HIGHEST

GRID_W = 64
HG_HEADS = 8
HG_DK = 128
RW_HEADS = 16
RW_DK = 64
N_EXPERTS = 32
TOP_K = 4
SWIGLU_LIMIT = 7.0
SWIGLU_ALPHA = 1.702
NORM_EPS = 1e-6
RW_LNX_EPS = 1e-5 * RW_DK

CHUNK = 64
SUB = 16
SEQ_BLOCK = 256
MOE_BLOCK = 256
VMEM_LIMIT = 56 * 1024 * 1024

_NT = (((1,), (1,)), ((), ()))


def _dot(a, b):
    return jnp.dot(a.astype(BF16), b.astype(BF16), preferred_element_type=F32)


def _dot_nt(a, b):
    return lax.dot_general(a.astype(BF16), b.astype(BF16), _NT, preferred_element_type=F32)


def _split(a):
    hi = a.astype(BF16)
    lo = (a - hi.astype(F32)).astype(BF16)
    return hi, lo


def _dot3(a, b, nt=False):
    a_hi, a_lo = _split(a)
    b_hi, b_lo = _split(b)
    if nt:
        f = lambda x, y: lax.dot_general(x, y, _NT, preferred_element_type=F32)
    else:
        f = lambda x, y: jnp.dot(x, y, preferred_element_type=F32)
    return f(a_hi, b_hi) + (f(a_hi, b_lo) + f(a_lo, b_hi))


def _mm_body(x_ref, w_ref, b_ref, o_ref, wb_ref):
    @pl.when(pl.program_id(1) == 0)
    def _():
        wb_ref[...] = w_ref[...].astype(BF16)

    acc = jnp.dot(x_ref[...].astype(BF16), wb_ref[...], preferred_element_type=F32)
    o_ref[...] = (acc + b_ref[...]).astype(o_ref.dtype)


def _mm_exact_body(x_ref, w_ref, b_ref, o_ref):
    acc = jnp.dot(x_ref[...], w_ref[...], precision=HIGHEST, preferred_element_type=F32)
    o_ref[...] = acc + b_ref[...]


def _mm(x, w, b=None, out_dtype=F32, tm=512, tn=512):
    m, k = x.shape
    n = w.shape[1]
    tm = min(tm, m)
    tn = min(tn, n)
    assert m % tm == 0 and n % tn == 0, (m, n, tm, tn)
    if b is None:
        b = jnp.zeros((n,), F32)
    return pl.pallas_call(
        _mm_body,
        grid=(n // tn, m // tm),
        in_specs=[
            pl.BlockSpec((tm, k), lambda j, i: (i, 0)),
            pl.BlockSpec((k, tn), lambda j, i: (0, j)),
            pl.BlockSpec((1, tn), lambda j, i: (0, j)),
        ],
        out_specs=pl.BlockSpec((tm, tn), lambda j, i: (i, j)),
        out_shape=jax.ShapeDtypeStruct((m, n), out_dtype),
        scratch_shapes=[pltpu.VMEM((k, tn), BF16)],
        compiler_params=pltpu.CompilerParams(
            dimension_semantics=("arbitrary", "arbitrary"), vmem_limit_bytes=VMEM_LIMIT),
        name="mm",
    )(x, w, b.reshape(1, n).astype(F32))


def _mm_exact(x, w, b, tn=512):
    m, k = x.shape
    n = w.shape[1]
    tn = min(tn, n)
    return pl.pallas_call(
        _mm_exact_body,
        grid=(n // tn,),
        in_specs=[
            pl.BlockSpec((m, k), lambda j: (0, 0)),
            pl.BlockSpec((k, tn), lambda j: (0, j)),
            pl.BlockSpec((1, tn), lambda j: (0, j)),
        ],
        out_specs=pl.BlockSpec((m, tn), lambda j: (0, j)),
        out_shape=jax.ShapeDtypeStruct((m, n), F32),
        compiler_params=pltpu.CompilerParams(
            dimension_semantics=("arbitrary",), vmem_limit_bytes=VMEM_LIMIT),
        name="mm_exact",
    )(x, w, b.reshape(1, n))


def _seq_block_index(c, n_blocks, n_ctx_blocks, rev):
    if not rev:
        return c
    return jnp.where(c < n_ctx_blocks, n_ctx_blocks - 1 - c, n_blocks - 1 - (c - n_ctx_blocks))


HG_HB = 4


def _gla_chunk(q, f, v, lb, sts, rev, consts):
    tri, ones_kk, trow = consts
    heads = range(len(sts))
    hs = [slice(h * HG_DK, (h + 1) * HG_DK) for h in heads]
    q = q * jax.nn.sigmoid(q)
    g = jnp.log(lb + (1.0 - lb) * jax.nn.sigmoid(f))
    k = (1.0 - lb) * jax.nn.sigmoid(-f)
    b = jnp.dot(tri, g, precision=HIGHEST, preferred_element_type=F32)
    b_end = b[0:1] if rev else b[CHUNK - 1:CHUNK]
    q_in = q * jnp.exp(b)
    k_end = k * jnp.exp(b_end - b)
    o_inter = [_dot_nt(q_in[:, s], sts[h]) for h, s in zip(heads, hs)]
    kv = [_dot(v[:, s].T, k_end[:, s]) for s in hs]
    e_end = jnp.exp(b_end)
    sts_new = [sts[h] * e_end[:, hs[h]] + kv[h] for h in heads]

    outs = [None] * (CHUNK // SUB)
    for p in range(CHUNK // SUB):
        if rev:
            r0 = CHUNK - SUB * (p + 1)
            e0, e1, mrow = CHUNK - SUB * p, CHUNK, CHUNK - SUB * p
        else:
            r0 = SUB * p
            e0, e1, mrow = 0, SUB * p, SUB * p - 1
        qp, bp, kp, vp = (x[r0:r0 + SUB] for x in (q, b, k, v))
        acc = [o_inter[h][r0:r0 + SUB] for h in heads]
        if p > 0:
            m = b[mrow:mrow + 1]
            q_lo = qp * jnp.exp(bp - m)
            k_hi = k[e0:e1] * jnp.exp(m - b[e0:e1])
            sc = [_dot_nt(q_lo[:, s], k_hi[:, s]) for s in hs]
            acc = [acc[h] + _dot(sc[h], v[e0:e1, hs[h]]) for h in heads]
        terms = []
        for s in range(SUB):
            mask = (trow <= s) if rev else (trow >= s)
            w = jnp.exp(jnp.where(mask, bp - bp[s:s + 1], -1e30))
            terms.append((qp * kp[s:s + 1] * w).astype(BF16))
        terms = jnp.concatenate(terms, axis=0)
        sums = [jnp.dot(terms[:, s], ones_kk, preferred_element_type=F32) for s in hs]
        for s in range(SUB):
            acc = [acc[h] + sums[h][s * SUB:(s + 1) * SUB] * vp[s:s + 1, hs[h]] for h in heads]
        outs[r0 // SUB] = jnp.concatenate(acc, axis=1)
    return jnp.concatenate(outs, axis=0), sts_new


def _gla_body(q_ref, f_ref, v_ref, lb_ref, o_ref, st_ref, *, rev):
    @pl.when(pl.program_id(2) == 0)
    def _():
        st_ref[...] = jnp.zeros_like(st_ref)

    row = lax.broadcasted_iota(jnp.int32, (CHUNK, CHUNK), 0)
    col = lax.broadcasted_iota(jnp.int32, (CHUNK, CHUNK), 1)
    tri = ((row <= col) if rev else (row >= col)).astype(F32)
    ones_kk = jnp.ones((HG_DK, HG_DK), BF16)
    trow = lax.broadcasted_iota(jnp.int32, (SUB, HG_HB * HG_DK), 0)
    lb = lb_ref[0]
    n_chunks = SEQ_BLOCK // CHUNK

    def step(i, carry):
        cc = (n_chunks - 1 - i) if rev else i
        sl = pl.ds(pl.multiple_of(cc * CHUNK, CHUNK), CHUNK)
        o, sts = _gla_chunk(q_ref[0, sl, :], f_ref[0, sl, :], v_ref[0, sl, :], lb,
                            [st_ref[h] for h in range(HG_HB)], rev, (tri, ones_kk, trow))
        o_ref[0, sl, :] = o
        for h in range(HG_HB):
            st_ref[h] = sts[h]
        return carry

    lax.fori_loop(0, n_chunks, step, 0)


def _gla_scan(z, lb, n_ctx, rev):
    bsz, t, _ = z.shape
    assert t % SEQ_BLOCK == 0 and n_ctx % SEQ_BLOCK == 0
    nb = t // SEQ_BLOCK
    d_model = HG_HEADS * HG_DK
    ng = HG_HEADS // HG_HB
    width = HG_HB * HG_DK
    f_off = ng * (2 if rev else 1)
    v_off = ng * 3
    tmap = lambda c: _seq_block_index(c, nb, n_ctx // SEQ_BLOCK, rev)
    return pl.pallas_call(
        functools.partial(_gla_body, rev=rev),
        grid=(bsz, ng, nb),
        in_specs=[
            pl.BlockSpec((1, SEQ_BLOCK, width), lambda b, h, c: (b, tmap(c), h)),
            pl.BlockSpec((1, SEQ_BLOCK, width), lambda b, h, c: (b, tmap(c), f_off + h)),
            pl.BlockSpec((1, SEQ_BLOCK, width), lambda b, h, c: (b, tmap(c), v_off + h)),
            pl.BlockSpec((1, 1, width), lambda b, h, c: (h, 0, 0)),
        ],
        out_specs=pl.BlockSpec((1, SEQ_BLOCK, width), lambda b, h, c: (b, tmap(c), h)),
        out_shape=jax.ShapeDtypeStruct((bsz, t, d_model), F32),
        scratch_shapes=[pltpu.VMEM((HG_HB, HG_DK, HG_DK), F32)],
        compiler_params=pltpu.CompilerParams(
            dimension_semantics=("arbitrary", "arbitrary", "arbitrary"),
            vmem_limit_bytes=VMEM_LIMIT),
        name="gla_rev" if rev else "gla_fwd",
    )(z, z, z, lb.reshape(ng, 1, width))


RW_HB = 8


def _rwkv_chunk(lw, cum, r, k, v, kap, a, sts, rev, consts):
    incl, strict, eye = consts
    nh = len(sts)
    heads = range(nh)
    hs = [slice(h * RW_DK, (h + 1) * RW_DK) for h in heads]
    cum_ex = cum - lw
    tot = cum[0:1] if rev else cum[CHUNK - 1:CHUNK]
    e_neg = jnp.exp(-cum)
    kt = kap * jnp.exp(cum_ex)
    bb = kap * a
    bt = bb * e_neg
    kd = k * e_neg
    rt = r * jnp.exp(cum)
    dec_end = jnp.exp(tot - cum)
    b_end = bb * dec_end
    k_end = k * dec_end
    e_tot = jnp.exp(tot)

    gram = [_dot_nt(jnp.concatenate([kt[:, s], rt[:, s]], axis=0),
                    jnp.concatenate([bt[:, s], kd[:, s]], axis=0)) for s in hs]
    a_bk = [jnp.where(strict, g[:CHUNK, :CHUNK], 0.0) for g in gram]
    a_kr = [jnp.concatenate([jnp.where(strict, g[:CHUNK, CHUNK:], 0.0),
                             jnp.where(incl, g[CHUNK:, CHUNK:], 0.0)], axis=0) for g in gram]
    a_rb = [jnp.where(incl, g[CHUNK:, :CHUNK], 0.0) for g in gram]
    av = [_dot(a_kr[h], v[:, hs[h]]) for h in heads]

    tinv = [eye - x for x in a_bk]
    pw = [_dot(x, x) for x in a_bk]
    for _ in range(4):
        tp = [_dot(jnp.concatenate([tinv[h], pw[h]], axis=0), pw[h]) for h in heads]
        tinv = [tinv[h] + tp[h][:CHUNK] for h in heads]
        pw = [x[CHUNK:] for x in tp]
    tinv = [tinv[h] + _dot(tinv[h], pw[h]) for h in heads]

    ku = [_dot(tinv[h], jnp.concatenate([kt[:, hs[h]], av[h][:CHUNK]], axis=1)) for h in heads]
    ab = [_dot(a_rb[h], ku[h]) for h in heads]
    rh = [rt[:, hs[h]] - ab[h][:, :RW_DK] for h in heads]
    y0 = [av[h][CHUNK:] - ab[h][:, RW_DK:] for h in heads]
    ys = [_dot_nt(rh[h], sts[h]) + y0[h] for h in heads]
    mb = [_dot(ku[h].T, b_end[:, hs[h]]) for h in heads]
    m_mat = [jnp.where(eye > 0, e_tot[:, hs[h]], 0.0) - mb[h][:RW_DK] for h in heads]
    n_mat = [_dot(v[:, hs[h]].T, k_end[:, hs[h]]) - mb[h][RW_DK:] for h in heads]
    sts_new = [_dot3(sts[h], m_mat[h]) + n_mat[h] for h in heads]
    return jnp.concatenate(ys, axis=1), sts_new


def _rwkv_body(lw_ref, r_ref, k_ref, v_ref, kap_ref, a_ref, y_ref, st_ref, *, rev):
    @pl.when(pl.program_id(2) == 0)
    def _():
        st_ref[...] = jnp.zeros_like(st_ref)

    row = lax.broadcasted_iota(jnp.int32, (CHUNK, CHUNK), 0)
    col = lax.broadcasted_iota(jnp.int32, (CHUNK, CHUNK), 1)
    incl = (row <= col) if rev else (row >= col)
    strict = (row < col) if rev else (row > col)
    eye = (row == col).astype(F32)
    tri = incl.astype(F32)
    n_chunks = SEQ_BLOCK // CHUNK

    def step(i, carry):
        cc = (n_chunks - 1 - i) if rev else i
        sl = pl.ds(pl.multiple_of(cc * CHUNK, CHUNK), CHUNK)
        lw = lw_ref[0, sl, :]
        cum = jnp.dot(tri, lw, precision=HIGHEST, preferred_element_type=F32)
        r, k, v, kap, a = (ref[0, sl, :] for ref in (r_ref, k_ref, v_ref, kap_ref, a_ref))
        y, sts = _rwkv_chunk(lw, cum, r, k, v, kap, a, [st_ref[h] for h in range(RW_HB)],
                             rev, (incl, strict, eye))
        for h in range(RW_HB):
            st_ref[h] = sts[h]
        y_ref[0, sl, :] = y
        return carry

    lax.fori_loop(0, n_chunks, step, 0)


def _rwkv_scan(lw, r, k, v, kap, a, n_ctx, rev):
    bsz, t, d_model = lw.shape
    assert t % SEQ_BLOCK == 0 and n_ctx % SEQ_BLOCK == 0
    nb = t // SEQ_BLOCK
    width = RW_HB * RW_DK
    tmap = lambda c: _seq_block_index(c, nb, n_ctx // SEQ_BLOCK, rev)
    spec = pl.BlockSpec((1, SEQ_BLOCK, width), lambda b, h, c: (b, tmap(c), h))
    return pl.pallas_call(
        functools.partial(_rwkv_body, rev=rev),
        grid=(bsz, d_model // width, nb),
        in_specs=[spec] * 6,
        out_specs=spec,
        out_shape=jax.ShapeDtypeStruct((bsz, t, d_model), F32),
        scratch_shapes=[pltpu.VMEM((RW_HB, RW_DK, RW_DK), F32)],
        compiler_params=pltpu.CompilerParams(
            dimension_semantics=("arbitrary", "arbitrary", "arbitrary"),
            vmem_limit_bytes=VMEM_LIMIT),
        name="rwkv_rev" if rev else "rwkv_fwd",
    )(lw, r, k, v, kap, a)


def _router_body(x_ref, w_ref, b_ref, idx_ref, gate_ref):
    logits = jnp.dot(x_ref[...], w_ref[...], precision=HIGHEST,
                     preferred_element_type=F32) + b_ref[...]
    lane = lax.broadcasted_iota(jnp.int32, logits.shape, 1)
    vals, idxs = [], []
    for _ in range(TOP_K):
        m = jnp.max(logits, axis=-1, keepdims=True)
        idx = jnp.min(jnp.where(logits == m, lane, N_EXPERTS), axis=-1, keepdims=True)
        vals.append(m)
        idxs.append(idx)
        logits = jnp.where(lane == idx, -jnp.inf, logits)
    exps = [jnp.exp(val - vals[0]) for val in vals]
    denom = exps[0] + exps[1] + exps[2] + exps[3]
    idx_ref[...] = jnp.concatenate(idxs, axis=1)
    gate_ref[...] = jnp.concatenate(exps, axis=1) / denom


def _router(x, w, b, tm=512):
    n, d = x.shape
    return pl.pallas_call(
        _router_body,
        grid=(n // tm,),
        in_specs=[
            pl.BlockSpec((tm, d), lambda i: (i, 0)),
            pl.BlockSpec((d, N_EXPERTS), lambda i: (0, 0)),
            pl.BlockSpec((1, N_EXPERTS), lambda i: (0, 0)),
        ],
        out_specs=[pl.BlockSpec((tm, TOP_K), lambda i: (i, 0)),
                   pl.BlockSpec((tm, TOP_K), lambda i: (i, 0))],
        out_shape=[jax.ShapeDtypeStruct((n, TOP_K), jnp.int32),
                   jax.ShapeDtypeStruct((n, TOP_K), F32)],
        compiler_params=pltpu.CompilerParams(
            dimension_semantics=("arbitrary",), vmem_limit_bytes=VMEM_LIMIT),
        name="router",
    )(x, w, b.reshape(1, N_EXPERTS))


EXPERT_COLS = 512


def _expert_body(be_ref, nu_ref, x_ref, wgu_ref, bgu_ref, wd_ref, bd_ref, g_ref, o_ref,
                 wgu_b, wd_b):
    i = pl.program_id(0)
    d_exp = wd_ref.shape[1]
    prev = be_ref[jnp.maximum(i - 1, 0)]

    @pl.when((i == 0) | (be_ref[i] != prev))
    def _():
        wgu_b[...] = wgu_ref[0].astype(BF16)
        wd_b[...] = wd_ref[0].astype(BF16)

    @pl.when(i < nu_ref[0])
    def _():
        x = x_ref[...]
        acc = jnp.zeros(o_ref.shape, F32)
        for c0 in range(0, d_exp, EXPERT_COLS):
            gate = jnp.dot(x, wgu_b[:, c0:c0 + EXPERT_COLS], preferred_element_type=F32)
            gate = gate + bgu_ref[0, :, c0:c0 + EXPERT_COLS]
            up = jnp.dot(x, wgu_b[:, d_exp + c0:d_exp + c0 + EXPERT_COLS],
                         preferred_element_type=F32)
            up = up + bgu_ref[0, :, d_exp + c0:d_exp + c0 + EXPERT_COLS]
            gate = jnp.minimum(gate, SWIGLU_LIMIT)
            up = jnp.clip(up, -SWIGLU_LIMIT, SWIGLU_LIMIT)
            act = gate * jax.nn.sigmoid(SWIGLU_ALPHA * gate) * (up + 1.0)
            acc = acc + jnp.dot(act.astype(BF16), wd_b[c0:c0 + EXPERT_COLS, :],
                                preferred_element_type=F32)
        o_ref[...] = (acc + bd_ref[0]) * g_ref[...]

    @pl.when(i >= nu_ref[0])
    def _():
        o_ref[...] = jnp.zeros_like(o_ref)


def _experts(xs, slot_gate, block_e, n_used, w_gu, b_gu, w_down, b_down):
    cap, d = xs.shape
    n_e, _, d2 = w_gu.shape
    d_exp = d2 // 2
    nb = cap // MOE_BLOCK
    grid_spec = pltpu.PrefetchScalarGridSpec(
        num_scalar_prefetch=2,
        grid=(nb,),
        in_specs=[
            pl.BlockSpec((MOE_BLOCK, d), lambda i, be, nu: (i, 0)),
            pl.BlockSpec((1, d, d2), lambda i, be, nu: (be[i], 0, 0)),
            pl.BlockSpec((1, 1, d2), lambda i, be, nu: (be[i], 0, 0)),
            pl.BlockSpec((1, d_exp, d), lambda i, be, nu: (be[i], 0, 0)),
            pl.BlockSpec((1, 1, d), lambda i, be, nu: (be[i], 0, 0)),
            pl.BlockSpec((MOE_BLOCK, 1), lambda i, be, nu: (i, 0)),
        ],
        out_specs=pl.BlockSpec((MOE_BLOCK, d), lambda i, be, nu: (i, 0)),
        scratch_shapes=[pltpu.VMEM((d, d2), BF16), pltpu.VMEM((d_exp, d), BF16)],
    )
    return pl.pallas_call(
        _expert_body,
        grid_spec=grid_spec,
        out_shape=jax.ShapeDtypeStruct((cap, d), F32),
        compiler_params=pltpu.CompilerParams(
            dimension_semantics=("arbitrary",), vmem_limit_bytes=VMEM_LIMIT),
        name="experts",
    )(block_e, n_used, xs, w_gu, b_gu.reshape(n_e, 1, d2), w_down, b_down.reshape(n_e, 1, d),
      slot_gate.reshape(cap, 1))


def _moe(tokens, router_w, router_b, w_gu, b_gu, w_down, b_down):
    n, d = tokens.shape
    top_idx, gates = _router(tokens, router_w, router_b)
    flat_e = top_idx.reshape(-1)
    onehot = (flat_e[:, None] == jnp.arange(N_EXPERTS, dtype=jnp.int32)[None, :]).astype(jnp.int32)
    rank = jnp.sum((jnp.cumsum(onehot, axis=0) - 1) * onehot, axis=1)
    counts = jnp.sum(onehot, axis=0)
    padded = (counts + MOE_BLOCK - 1) // MOE_BLOCK * MOE_BLOCK
    padded_end = jnp.cumsum(padded)
    padded_start = padded_end - padded
    dest = padded_start[flat_e] + rank
    n_blocks = -(-(n * TOP_K + N_EXPERTS * (MOE_BLOCK - 1)) // MOE_BLOCK)
    cap = n_blocks * MOE_BLOCK
    flat_tok = jnp.repeat(jnp.arange(n, dtype=jnp.int32), TOP_K)
    slot_tok = jnp.full((cap,), n, jnp.int32).at[dest].set(flat_tok)
    slot_gate = jnp.zeros((cap,), F32).at[dest].set(gates.reshape(-1))
    n_used = (padded_end[-1] // MOE_BLOCK).astype(jnp.int32)
    blk = jnp.minimum(jnp.arange(n_blocks, dtype=jnp.int32), n_used - 1) * MOE_BLOCK
    block_e = jnp.minimum(jnp.sum((padded_end[None, :] <= blk[:, None]).astype(jnp.int32), axis=1),
                          N_EXPERTS - 1)
    x_pad = jnp.concatenate([tokens.astype(BF16), jnp.zeros((1, d), BF16)], axis=0)
    xs = x_pad[slot_tok]
    ys = _experts(xs, slot_gate, block_e, n_used.reshape(1), w_gu, b_gu, w_down, b_down)
    return jnp.sum(ys[dest].reshape(n, TOP_K, d), axis=1)


def _rmsnorm(x, g):
    return x * lax.rsqrt(jnp.mean(x * x, axis=-1, keepdims=True) + NORM_EPS) * g


def _mm_tokens(h, w, b=None, **kw):
    lead = h.shape[:-1]
    return _mm(h.reshape(-1, h.shape[-1]), w, b, **kw).reshape(lead + (w.shape[1],))


def _shift_grid(h):
    bsz, t, d = h.shape
    g = h.reshape(bsz, t // GRID_W, GRID_W, d)
    c = d // 4
    left = jnp.pad(g[:, :, :-1, :c], ((0, 0), (0, 0), (1, 0), (0, 0)))
    right = jnp.pad(g[:, :, 1:, c:2 * c], ((0, 0), (0, 0), (0, 1), (0, 0)))
    up = jnp.pad(g[:, :-1, :, 2 * c:3 * c], ((0, 0), (1, 0), (0, 0), (0, 0)))
    down = jnp.pad(g[:, 1:, :, 3 * c:], ((0, 0), (0, 1), (0, 0), (0, 0)))
    return jnp.concatenate([left, right, up, down], axis=-1).reshape(bsz, t, d)


def _shift_seq(h):
    c = h.shape[-1] // 2
    prev = jnp.pad(h[:, :-1, :c], ((0, 0), (1, 0), (0, 0)))
    nxt = jnp.pad(h[:, 1:, c:], ((0, 0), (0, 1), (0, 0)))
    return jnp.concatenate([prev, nxt], axis=-1)


def _hgrn2_mixer(h, n_ctx, w_in, gnorm_w, w_out, lb):
    bsz, t, d = h.shape
    z = _mm_tokens(h.astype(BF16), w_in)
    o = _gla_scan(z, lb[0], n_ctx, False) + _gla_scan(z, lb[1], n_ctx, True)
    g = z[..., 4 * d:]
    oh = o.reshape(bsz, t, HG_HEADS, HG_DK)
    oh = _rmsnorm(oh, gnorm_w) * jax.nn.silu(g.reshape(bsz, t, HG_HEADS, HG_DK))
    return _mm_tokens(oh.reshape(bsz, t, d).astype(BF16), w_out)


def _rwkv7_mixer(h, n_ctx, mu, w_rkv, dec_w0, dec_w1, dec_w2, iclr_a0, iclr_a1, iclr_a2,
                 g1, g2, k_k, k_a, r_k, lnx_w, lnx_b, w_out):
    bsz, t, d = h.shape
    h_shift = jnp.concatenate([_shift_seq(h[:, :n_ctx]), _shift_grid(h[:, n_ctx:])], axis=1)
    dx = h_shift - h
    xr, xw, xk, xv, xa, xg = ((h + dx * mu[j]).astype(BF16) for j in range(6))
    heads = lambda a: a.reshape(bsz, t, RW_HEADS, RW_DK)
    r = _mm_tokens(xr, w_rkv[0])
    k = _mm_tokens(xk, w_rkv[1])
    v = _mm_tokens(xv, w_rkv[2])
    kk = heads(k * k_k)
    kk = (kk / jnp.maximum(jnp.sqrt(jnp.sum(kk * kk, axis=-1, keepdims=True)), 1e-12)).reshape(bsz, t, d)
    lora_w = jnp.tanh(_mm_tokens(xw, jnp.concatenate([dec_w1[0], dec_w1[1]], axis=1)))
    lora_a = _mm_tokens(xa, jnp.concatenate([iclr_a1[0], iclr_a1[1]], axis=1))
    nw = dec_w1.shape[-1]
    na = iclr_a1.shape[-1]
    y = 0.0
    bonus = 0.0
    for dd, rev in enumerate((False, True)):
        zw = _mm_tokens(lora_w[..., dd * nw:(dd + 1) * nw], dec_w2[dd], dec_w0[dd])
        lw = -jnp.exp(-jax.nn.softplus(-zw) - 0.5)
        a = jax.nn.sigmoid(_mm_tokens(lora_a[..., dd * na:(dd + 1) * na], iclr_a2[dd], iclr_a0[dd]))
        k_dir = k * (1.0 + (a - 1.0) * k_a)
        y = y + _rwkv_scan(lw, r, k_dir, v, kk, a, n_ctx, rev)
        bonus = bonus + jnp.sum(heads(r * k_dir) * r_k, axis=-1, keepdims=True)
    gate = _mm_tokens(jax.nn.sigmoid(_mm_tokens(xg, g1)), g2)
    yh = heads(y)
    mean = jnp.mean(yh, axis=-1, keepdims=True)
    var = jnp.mean(jnp.square(yh - mean), axis=-1, keepdims=True)
    yn = (yh - mean) * lax.rsqrt(var + RW_LNX_EPS)
    yn = yn * lnx_w.reshape(RW_HEADS, RW_DK) + lnx_b.reshape(RW_HEADS, RW_DK)
    yn = yn + bonus * heads(v)
    return _mm_tokens((yn.reshape(bsz, t, d) * gate).astype(BF16), w_out)


def kernel(x, c, ctx, c_ctx, ada_w, ada_b, norm_mix_g, norm_ffn_g, hg_w_in, hg_gnorm_w, hg_w_out, hg_lb, rw_mu, rw_w_rkv, rw_dec_w0, rw_dec_w1, rw_dec_w2, rw_iclr_a0, rw_iclr_a1, rw_iclr_a2, rw_g1, rw_g2, rw_k_k, rw_k_a, rw_r_k, rw_lnx_w, rw_lnx_b, rw_w_out, moe_router_w, moe_router_b, moe_w_gu, moe_b_gu, moe_w_down, moe_b_down, final_g):
    bsz, n_lat, d = x.shape
    n_ctx = ctx.shape[1]
    depth = ada_w.shape[0]
    lb_all = jnp.cumsum(jax.nn.softmax(hg_lb.astype(F32), axis=1), axis=1)
    cond = jnp.concatenate([jax.nn.silu(c), jax.nn.silu(c_ctx)[None, :],
                            jnp.zeros((8 - bsz - 1, d), F32)], axis=0)
    xs = jnp.concatenate([ctx, x], axis=1)
    t = n_ctx + n_lat
    for layer in range(depth):
        mod = _mm_exact(cond, ada_w[layer], ada_b[layer])
        mod_l = mod[:bsz].reshape(bsz, 1, 6, d)
        mod_c = jnp.broadcast_to(mod[bsz].reshape(1, 1, 6, d), (bsz, 1, 6, d))
        seg = lambda j: jnp.concatenate(
            [jnp.broadcast_to(mod_c[:, :, j], (bsz, n_ctx, d)),
             jnp.broadcast_to(mod_l[:, :, j], (bsz, n_lat, d))], axis=1)
        h = _rmsnorm(xs, norm_mix_g[layer]) * (1.0 + seg(1)) + seg(0)
        j = layer // 2
        if layer % 2 == 0:
            y = _hgrn2_mixer(h, n_ctx, hg_w_in[j], hg_gnorm_w[j], hg_w_out[j], lb_all[:, layer])
        else:
            y = _rwkv7_mixer(h, n_ctx, rw_mu[j], rw_w_rkv[j], rw_dec_w0[j], rw_dec_w1[j],
                             rw_dec_w2[j], rw_iclr_a0[j], rw_iclr_a1[j], rw_iclr_a2[j],
                             rw_g1[j], rw_g2[j], rw_k_k[j], rw_k_a[j], rw_r_k[j],
                             rw_lnx_w[j], rw_lnx_b[j], rw_w_out[j])
        xs = xs + seg(2) * y
        h = _rmsnorm(xs, norm_ffn_g[layer]) * (1.0 + seg(4)) + seg(3)
        out = _moe(h.reshape(bsz * t, d), moe_router_w[layer], moe_router_b[layer],
                   moe_w_gu[layer], moe_b_gu[layer], moe_w_down[layer], moe_b_down[layer])
        xs = xs + seg(5) * out.reshape(bsz, t, d)
    return _rmsnorm(xs[:, n_ctx:], final_g)
```

```python
import functools

import jax
import jax.numpy as jnp
from jax import lax
from jax.experimental import pallas as pl
from jax.experimental.pallas import tpu as pltpu

F32 = jnp.float32
BF16 = jnp.bfloat16
HIGHEST = lax.Precision.HIGHEST

GRID_W = 64
HG_HEADS = 8
HG_DK = 128
RW_HEADS = 16
RW_DK = 64
N_EXPERTS = 32
TOP_K = 4
SWIGLU_LIMIT = 7.0
SWIGLU_ALPHA = 1.702
NORM_EPS = 1e-6
RW_LNX_EPS = 1e-5 * RW_DK

CHUNK = 64
SUB = 16
SEQ_BLOCK = 256
MOE_BLOCK = 256
VMEM_LIMIT = 56 * 1024 * 1024

_NT = (((1,), (1,)), ((), ()))


def _dot(a, b):
    return jnp.dot(a.astype(BF16), b.astype(BF16), preferred_element_type=F32)


def _dot_nt(a, b):
    return lax.dot_general(a.astype(BF16), b.astype(BF16), _NT, preferred_element_type=F32)


def _split(a):
    hi = a.astype(BF16)
    lo = (a - hi.astype(F32)).astype(BF16)
    return hi, lo


def _dot3(a, b, nt=False):
    a_hi, a_lo = _split(a)
    b_hi, b_lo = _split(b)
    if nt:
        f = lambda x, y: lax.dot_general(x, y, _NT, preferred_element_type=F32)
    else:
        f = lambda x, y: jnp.dot(x, y, preferred_element_type=F32)
    return f(a_hi, b_hi) + (f(a_hi, b_lo) + f(a_lo, b_hi))


def _mm_body(x_ref, w_ref, b_ref, o_ref, wb_ref):
    @pl.when(pl.program_id(1) == 0)
    def _():
        wb_ref[...] = w_ref[...].astype(BF16)

    acc = jnp.dot(x_ref[...].astype(BF16), wb_ref[...], preferred_element_type=F32)
    o_ref[...] = (acc + b_ref[...]).astype(o_ref.dtype)


def _mm_exact_body(x_ref, w_ref, b_ref, o_ref):
    acc = jnp.dot(x_ref[...], w_ref[...], precision=HIGHEST, preferred_element_type=F32)
    o_ref[...] = acc + b_ref[...]


def _mm(x, w, b=None, out_dtype=F32, tm=512, tn=512):
    m, k = x.shape
    n = w.shape[1]
    tm = min(tm, m)
    tn = min(tn, n)
    assert m % tm == 0 and n % tn == 0, (m, n, tm, tn)
    if b is None:
        b = jnp.zeros((n,), F32)
    return pl.pallas_call(
        _mm_body,
        grid=(n // tn, m // tm),
        in_specs=[
            pl.BlockSpec((tm, k), lambda j, i: (i, 0)),
            pl.BlockSpec((k, tn), lambda j, i: (0, j)),
            pl.BlockSpec((1, tn), lambda j, i: (0, j)),
        ],
        out_specs=pl.BlockSpec((tm, tn), lambda j, i: (i, j)),
        out_shape=jax.ShapeDtypeStruct((m, n), out_dtype),
        scratch_shapes=[pltpu.VMEM((k, tn), BF16)],
        compiler_params=pltpu.CompilerParams(
            dimension_semantics=("arbitrary", "arbitrary"), vmem_limit_bytes=VMEM_LIMIT),
        name="mm",
    )(x, w, b.reshape(1, n).astype(F32))


def _mm_exact(x, w, b, tn=512):
    m, k = x.shape
    n = w.shape[1]
    tn = min(tn, n)
    return pl.pallas_call(
        _mm_exact_body,
        grid=(n // tn,),
        in_specs=[
            pl.BlockSpec((m, k), lambda j: (0, 0)),
            pl.BlockSpec((k, tn), lambda j: (0, j)),
            pl.BlockSpec((1, tn), lambda j: (0, j)),
        ],
        out_specs=pl.BlockSpec((m, tn), lambda j: (0, j)),
        out_shape=jax.ShapeDtypeStruct((m, n), F32),
        compiler_params=pltpu.CompilerParams(
            dimension_semantics=("arbitrary",), vmem_limit_bytes=VMEM_LIMIT),
        name="mm_exact",
    )(x, w, b.reshape(1, n))


def _seq_block_index(c, n_blocks, n_ctx_blocks, rev):
    if not rev:
        return c
    return jnp.where(c < n_ctx_blocks, n_ctx_blocks - 1 - c, n_blocks - 1 - (c - n_ctx_blocks))


HG_HB = 4


def _gla_chunk(q, f, v, lb, sts, rev, consts):
    tri, ones_kk, trow = consts
    heads = range(len(sts))
    hs = [slice(h * HG_DK, (h + 1) * HG_DK) for h in heads]
    q = q * jax.nn.sigmoid(q)
    g = jnp.log(lb + (1.0 - lb) * jax.nn.sigmoid(f))
    k = (1.0 - lb) * jax.nn.sigmoid(-f)
    b = jnp.dot(tri, g, precision=HIGHEST, preferred_element_type=F32)
    b_end = b[0:1] if rev else b[CHUNK - 1:CHUNK]
    q_in = q * jnp.exp(b)
    k_end = k * jnp.exp(b_end - b)
    o_inter = [_dot_nt(q_in[:, s], sts[h]) for h, s in zip(heads, hs)]
    kv = [_dot(v[:, s].T, k_end[:, s]) for s in hs]
    e_end = jnp.exp(b_end)
    sts_new = [sts[h] * e_end[:, hs[h]] + kv[h] for h in heads]

    outs = [None] * (CHUNK // SUB)
    for p in range(CHUNK // SUB):
        if rev:
            r0 = CHUNK - SUB * (p + 1)
            e0, e1, mrow = CHUNK - SUB * p, CHUNK, CHUNK - SUB * p
        else:
            r0 = SUB * p
            e0, e1, mrow = 0, SUB * p, SUB * p - 1
        qp, bp, kp, vp = (x[r0:r0 + SUB] for x in (q, b, k, v))
        acc = [o_inter[h][r0:r0 + SUB] for h in heads]
        if p > 0:
            m = b[mrow:mrow + 1]
            q_lo = qp * jnp.exp(bp - m)
            k_hi = k[e0:e1] * jnp.exp(m - b[e0:e1])
            sc = [_dot_nt(q_lo[:, s], k_hi[:, s]) for s in hs]
            acc = [acc[h] + _dot(sc[h], v[e0:e1, hs[h]]) for h in heads]
        terms = []
        for s in range(SUB):
            mask = (trow <= s) if rev else (trow >= s)
            w = jnp.exp(jnp.where(mask, bp - bp[s:s + 1], -1e30))
            terms.append((qp * kp[s:s + 1] * w).astype(BF16))
        terms = jnp.concatenate(terms, axis=0)
        sums = [jnp.dot(terms[:, s], ones_kk, preferred_element_type=F32) for s in hs]
        for s in range(SUB):
            acc = [acc[h] + sums[h][s * SUB:(s + 1) * SUB] * vp[s:s + 1, hs[h]] for h in heads]
        outs[r0 // SUB] = jnp.concatenate(acc, axis=1)
    return jnp.concatenate(outs, axis=0), sts_new


def _gla_body(q_ref, f_ref, v_ref, lb_ref, o_ref, st_ref, *, rev):
    @pl.when(pl.program_id(2) == 0)
    def _():
        st_ref[...] = jnp.zeros_like(st_ref)

    row = lax.broadcasted_iota(jnp.int32, (CHUNK, CHUNK), 0)
    col = lax.broadcasted_iota(jnp.int32, (CHUNK, CHUNK), 1)
    tri = ((row <= col) if rev else (row >= col)).astype(F32)
    ones_kk = jnp.ones((HG_DK, HG_DK), BF16)
    trow = lax.broadcasted_iota(jnp.int32, (SUB, HG_HB * HG_DK), 0)
    lb = lb_ref[0]
    n_chunks = SEQ_BLOCK // CHUNK

    def step(i, carry):
        cc = (n_chunks - 1 - i) if rev else i
        sl = pl.ds(pl.multiple_of(cc * CHUNK, CHUNK), CHUNK)
        o, sts = _gla_chunk(q_ref[0, sl, :], f_ref[0, sl, :], v_ref[0, sl, :], lb,
                            [st_ref[h] for h in range(HG_HB)], rev, (tri, ones_kk, trow))
        o_ref[0, sl, :] = o
        for h in range(HG_HB):
            st_ref[h] = sts[h]
        return carry

    lax.fori_loop(0, n_chunks, step, 0)


def _gla_scan(z, lb, n_ctx, rev):
    bsz, t, _ = z.shape
    assert t % SEQ_BLOCK == 0 and n_ctx % SEQ_BLOCK == 0
    nb = t // SEQ_BLOCK
    d_model = HG_HEADS * HG_DK
    ng = HG_HEADS // HG_HB
    width = HG_HB * HG_DK
    f_off = ng * (2 if rev else 1)
    v_off = ng * 3
    tmap = lambda c: _seq_block_index(c, nb, n_ctx // SEQ_BLOCK, rev)
    return pl.pallas_call(
        functools.partial(_gla_body, rev=rev),
        grid=(bsz, ng, nb),
        in_specs=[
            pl.BlockSpec((1, SEQ_BLOCK, width), lambda b, h, c: (b, tmap(c), h)),
            pl.BlockSpec((1, SEQ_BLOCK, width), lambda b, h, c: (b, tmap(c), f_off + h)),
            pl.BlockSpec((1, SEQ_BLOCK, width), lambda b, h, c: (b, tmap(c), v_off + h)),
            pl.BlockSpec((1, 1, width), lambda b, h, c: (h, 0, 0)),
        ],
        out_specs=pl.BlockSpec((1, SEQ_BLOCK, width), lambda b, h, c: (b, tmap(c), h)),
        out_shape=jax.ShapeDtypeStruct((bsz, t, d_model), F32),
        scratch_shapes=[pltpu.VMEM((HG_HB, HG_DK, HG_DK), F32)],
        compiler_params=pltpu.CompilerParams(
            dimension_semantics=("arbitrary", "arbitrary", "arbitrary"),
            vmem_limit_bytes=VMEM_LIMIT),
        name="gla_rev" if rev else "gla_fwd",
    )(z, z, z, lb.reshape(ng, 1, width))


RW_HB = 8


def _rwkv_chunk(lw, cum, r, k, v, kap, a, sts, rev, consts):
    incl, strict, eye = consts
    nh = len(sts)
    heads = range(nh)
    hs = [slice(h * RW_DK, (h + 1) * RW_DK) for h in heads]
    cum_ex = cum - lw
    tot = cum[0:1] if rev else cum[CHUNK - 1:CHUNK]
    e_neg = jnp.exp(-cum)
    kt = kap * jnp.exp(cum_ex)
    bb = kap * a
    bt = bb * e_neg
    kd = k * e_neg
    rt = r * jnp.exp(cum)
    dec_end = jnp.exp(tot - cum)
    b_end = bb * dec_end
    k_end = k * dec_end
    e_tot = jnp.exp(tot)

    gram = [_dot_nt(jnp.concatenate([kt[:, s], rt[:, s]], axis=0),
                    jnp.concatenate([bt[:, s], kd[:, s]], axis=0)) for s in hs]
    a_bk = [jnp.where(strict, g[:CHUNK, :CHUNK], 0.0) for g in gram]
    a_kr = [jnp.concatenate([jnp.where(strict, g[:CHUNK, CHUNK:], 0.0),
                             jnp.where(incl, g[CHUNK:, CHUNK:], 0.0)], axis=0) for g in gram]
    a_rb = [jnp.where(incl, g[CHUNK:, :CHUNK], 0.0) for g in gram]
    av = [_dot(a_kr[h], v[:, hs[h]]) for h in heads]

    tinv = [eye - x for x in a_bk]
    pw = [_dot(x, x) for x in a_bk]
    for _ in range(4):
        tp = [_dot(jnp.concatenate([tinv[h], pw[h]], axis=0), pw[h]) for h in heads]
        tinv = [tinv[h] + tp[h][:CHUNK] for h in heads]
        pw = [x[CHUNK:] for x in tp]
    tinv = [tinv[h] + _dot(tinv[h], pw[h]) for h in heads]

    ku = [_dot(tinv[h], jnp.concatenate([kt[:, hs[h]], av[h][:CHUNK]], axis=1)) for h in heads]
    ab = [_dot(a_rb[h], ku[h]) for h in heads]
    rh = [rt[:, hs[h]] - ab[h][:, :RW_DK] for h in heads]
    y0 = [av[h][CHUNK:] - ab[h][:, RW_DK:] for h in heads]
    ys = [_dot_nt(rh[h], sts[h]) + y0[h] for h in heads]
    mb = [_dot(ku[h].T, b_end[:, hs[h]]) for h in heads]
    m_mat = [jnp.where(eye > 0, e_tot[:, hs[h]], 0.0) - mb[h][:RW_DK] for h in heads]
    n_mat = [_dot(v[:, hs[h]].T, k_end[:, hs[h]]) - mb[h][RW_DK:] for h in heads]
    sts_new = [_dot3(sts[h], m_mat[h]) + n_mat[h] for h in heads]
    return jnp.concatenate(ys, axis=1), sts_new


def _rwkv_body(lw_ref, r_ref, k_ref, v_ref, kap_ref, a_ref, y_ref, st_ref, *, rev):
    @pl.when(pl.program_id(2) == 0)
    def _():
        st_ref[...] = jnp.zeros_like(st_ref)

    row = lax.broadcasted_iota(jnp.int32, (CHUNK, CHUNK), 0)
    col = lax.broadcasted_iota(jnp.int32, (CHUNK, CHUNK), 1)
    incl = (row <= col) if rev else (row >= col)
    strict = (row < col) if rev else (row > col)
    eye = (row == col).astype(F32)
    tri = incl.astype(F32)
    n_chunks = SEQ_BLOCK // CHUNK

    def step(i, carry):
        cc = (n_chunks - 1 - i) if rev else i
        sl = pl.ds(pl.multiple_of(cc * CHUNK, CHUNK), CHUNK)
        lw = lw_ref[0, sl, :]
        cum = jnp.dot(tri, lw, precision=HIGHEST, preferred_element_type=F32)
        r, k, v, kap, a = (ref[0, sl, :] for ref in (r_ref, k_ref, v_ref, kap_ref, a_ref))
        y, sts = _rwkv_chunk(lw, cum, r, k, v, kap, a, [st_ref[h] for h in range(RW_HB)],
                             rev, (incl, strict, eye))
        for h in range(RW_HB):
            st_ref[h] = sts[h]
        y_ref[0, sl, :] = y
        return carry

    lax.fori_loop(0, n_chunks, step, 0)


def _rwkv_scan(lw, r, k, v, kap, a, n_ctx, rev):
    bsz, t, d_model = lw.shape
    assert t % SEQ_BLOCK == 0 and n_ctx % SEQ_BLOCK == 0
    nb = t // SEQ_BLOCK
    width = RW_HB * RW_DK
    tmap = lambda c: _seq_block_index(c, nb, n_ctx // SEQ_BLOCK, rev)
    spec = pl.BlockSpec((1, SEQ_BLOCK, width), lambda b, h, c: (b, tmap(c), h))
    return pl.pallas_call(
        functools.partial(_rwkv_body, rev=rev),
        grid=(bsz, d_model // width, nb),
        in_specs=[spec] * 6,
        out_specs=spec,
        out_shape=jax.ShapeDtypeStruct((bsz, t, d_model), F32),
        scratch_shapes=[pltpu.VMEM((RW_HB, RW_DK, RW_DK), F32)],
        compiler_params=pltpu.CompilerParams(
            dimension_semantics=("arbitrary", "arbitrary", "arbitrary"),
            vmem_limit_bytes=VMEM_LIMIT),
        name="rwkv_rev" if rev else "rwkv_fwd",
    )(lw, r, k, v, kap, a)


def _router_body(x_ref, w_ref, b_ref, idx_ref, gate_ref):
    logits = jnp.dot(x_ref[...], w_ref[...], precision=HIGHEST,
                     preferred_element_type=F32) + b_ref[...]
    lane = lax.broadcasted_iota(jnp.int32, logits.shape, 1)
    vals, idxs = [], []
    for _ in range(TOP_K):
        m = jnp.max(logits, axis=-1, keepdims=True)
        idx = jnp.min(jnp.where(logits == m, lane, N_EXPERTS), axis=-1, keepdims=True)
        vals.append(m)
        idxs.append(idx)
        logits = jnp.where(lane == idx, -jnp.inf, logits)
    exps = [jnp.exp(val - vals[0]) for val in vals]
    denom = exps[0] + exps[1] + exps[2] + exps[3]
    idx_ref[...] = jnp.concatenate(idxs, axis=1)
    gate_ref[...] = jnp.concatenate(exps, axis=1) / denom


def _router(x, w, b, tm=512):
    n, d = x.shape
    return pl.pallas_call(
        _router_body,
        grid=(n // tm,),
        in_specs=[
            pl.BlockSpec((tm, d), lambda i: (i, 0)),
            pl.BlockSpec((d, N_EXPERTS), lambda i: (0, 0)),
            pl.BlockSpec((1, N_EXPERTS), lambda i: (0, 0)),
        ],
        out_specs=[pl.BlockSpec((tm, TOP_K), lambda i: (i, 0)),
                   pl.BlockSpec((tm, TOP_K), lambda i: (i, 0))],
        out_shape=[jax.ShapeDtypeStruct((n, TOP_K), jnp.int32),
                   jax.ShapeDtypeStruct((n, TOP_K), F32)],
        compiler_params=pltpu.CompilerParams(
            dimension_semantics=("arbitrary",), vmem_limit_bytes=VMEM_LIMIT),
        name="router",
    )(x, w, b.reshape(1, N_EXPERTS))


EXPERT_COLS = 512


def _expert_body(be_ref, nu_ref, x_ref, wgu_ref, bgu_ref, wd_ref, bd_ref, o_ref,
                 wgu_b, wd_b):
    i = pl.program_id(0)
    d_exp = wd_ref.shape[1]
    prev = be_ref[jnp.maximum(i - 1, 0)]

    @pl.when((i == 0) | (be_ref[i] != prev))
    def _():
        wgu_b[...] = wgu_ref[0].astype(BF16)
        wd_b[...] = wd_ref[0].astype(BF16)

    @pl.when(i < nu_ref[0])
    def _():
        x = x_ref[...].astype(BF16)
        acc = jnp.zeros(o_ref.shape, F32)
        for c0 in range(0, d_exp, EXPERT_COLS):
            gate = jnp.dot(x, wgu_b[:, c0:c0 + EXPERT_COLS], preferred_element_type=F32)
            gate = gate + bgu_ref[0, :, c0:c0 + EXPERT_COLS]
            up = jnp.dot(x, wgu_b[:, d_exp + c0:d_exp + c0 + EXPERT_COLS],
                         preferred_element_type=F32)
            up = up + bgu_ref[0, :, d_exp + c0:d_exp + c0 + EXPERT_COLS]
            gate = jnp.minimum(gate, SWIGLU_LIMIT)
            up = jnp.clip(up, -SWIGLU_LIMIT, SWIGLU_LIMIT)
            act = gate * jax.nn.sigmoid(SWIGLU_ALPHA * gate) * (up + 1.0)
            acc = acc + jnp.dot(act.astype(BF16), wd_b[c0:c0 + EXPERT_COLS, :],
                                preferred_element_type=F32)
        o_ref[...] = acc + bd_ref[0]

    @pl.when(i >= nu_ref[0])
    def _():
        o_ref[...] = jnp.zeros_like(o_ref)


def _experts(xs, block_e, n_used, w_gu, b_gu, w_down, b_down):
    cap, d = xs.shape
    n_e, _, d2 = w_gu.shape
    d_exp = d2 // 2
    nb = cap // MOE_BLOCK
    grid_spec = pltpu.PrefetchScalarGridSpec(
        num_scalar_prefetch=2,
        grid=(nb,),
        in_specs=[
            pl.BlockSpec((MOE_BLOCK, d), lambda i, be, nu: (i, 0)),
            pl.BlockSpec((1, d, d2), lambda i, be, nu: (be[i], 0, 0)),
            pl.BlockSpec((1, 1, d2), lambda i, be, nu: (be[i], 0, 0)),
            pl.BlockSpec((1, d_exp, d), lambda i, be, nu: (be[i], 0, 0)),
            pl.BlockSpec((1, 1, d), lambda i, be, nu: (be[i], 0, 0)),
        ],
        out_specs=pl.BlockSpec((MOE_BLOCK, d), lambda i, be, nu: (i, 0)),
        scratch_shapes=[pltpu.VMEM((d, d2), BF16), pltpu.VMEM((d_exp, d), BF16)],
    )
    return pl.pallas_call(
        _expert_body,
        grid_spec=grid_spec,
        out_shape=jax.ShapeDtypeStruct((cap, d), F32),
        compiler_params=pltpu.CompilerParams(
            dimension_semantics=("arbitrary",), vmem_limit_bytes=VMEM_LIMIT),
        name="experts",
    )(block_e, n_used, xs, w_gu, b_gu.reshape(n_e, 1, d2), w_down, b_down.reshape(n_e, 1, d))


ROUTE_BLOCK = 256


def _dispatch_body(dst_ref, x_ref, init_ref, xs_ref, sem):
    del init_ref

    def copy(r, j):
        return pltpu.make_async_copy(x_ref.at[pl.ds(r, 1)],
                                     xs_ref.at[pl.ds(dst_ref[0, 0, r * TOP_K + j], 1)], sem)

    def start(r, carry):
        for j in range(TOP_K):
            copy(r, j).start()
        return carry

    def wait(r, carry):
        for j in range(TOP_K):
            copy(r, j).wait()
        return carry

    lax.fori_loop(0, ROUTE_BLOCK, start, 0, unroll=8)
    lax.fori_loop(0, ROUTE_BLOCK, wait, 0, unroll=8)


def _dispatch(tokens, dest, cap):
    n, d = tokens.shape
    nblk = n // ROUTE_BLOCK
    return pl.pallas_call(
        _dispatch_body,
        grid=(nblk,),
        in_specs=[
            pl.BlockSpec((1, 1, ROUTE_BLOCK * TOP_K), lambda i: (i, 0, 0),
                         memory_space=pltpu.SMEM),
            pl.BlockSpec((ROUTE_BLOCK, d), lambda i: (i, 0)),
            pl.BlockSpec(memory_space=pl.ANY),
        ],
        out_specs=pl.BlockSpec(memory_space=pl.ANY),
        out_shape=jax.ShapeDtypeStruct((cap, d), F32),
        scratch_shapes=[pltpu.SemaphoreType.DMA(())],
        input_output_aliases={2: 0},
        compiler_params=pltpu.CompilerParams(
            dimension_semantics=("arbitrary",), vmem_limit_bytes=VMEM_LIMIT),
        name="dispatch",
    )(dest.reshape(nblk, 1, ROUTE_BLOCK * TOP_K), tokens, jnp.zeros((cap, d), F32))


def _combine_body(dst_ref, gate_ref, ys_ref, o_ref, buf, sem):
    def copy(r, j):
        return pltpu.make_async_copy(ys_ref.at[pl.ds(dst_ref[0, 0, r * TOP_K + j], 1)],
                                     buf.at[j, pl.ds(r, 1)], sem)

    def start(r, carry):
        for j in range(TOP_K):
            copy(r, j).start()
        return carry

    def wait(r, carry):
        for j in range(TOP_K):
            copy(r, j).wait()
        return carry

    lax.fori_loop(0, ROUTE_BLOCK, start, 0, unroll=8)
    lax.fori_loop(0, ROUTE_BLOCK, wait, 0, unroll=8)
    gates = gate_ref[...]
    acc = gates[:, 0:1] * buf[0]
    for j in range(1, TOP_K):
        acc = acc + gates[:, j:j + 1] * buf[j]
    o_ref[...] = acc


def _combine(ys, dest, gates):
    n = gates.shape[0]
    d = ys.shape[1]
    nblk = n // ROUTE_BLOCK
    return pl.pallas_call(
        _combine_body,
        grid=(nblk,),
        in_specs=[
            pl.BlockSpec((1, 1, ROUTE_BLOCK * TOP_K), lambda i: (i, 0, 0),
                         memory_space=pltpu.SMEM),
            pl.BlockSpec((ROUTE_BLOCK, TOP_K), lambda i: (i, 0)),
            pl.BlockSpec(memory_space=pl.ANY),
        ],
        out_specs=pl.BlockSpec((ROUTE_BLOCK, d), lambda i: (i, 0)),
        out_shape=jax.ShapeDtypeStruct((n, d), F32),
        scratch_shapes=[pltpu.VMEM((TOP_K, ROUTE_BLOCK, d), F32), pltpu.SemaphoreType.DMA(())],
        compiler_params=pltpu.CompilerParams(
            dimension_semantics=("arbitrary",), vmem_limit_bytes=VMEM_LIMIT),
        name="combine",
    )(dest.reshape(nblk, 1, ROUTE_BLOCK * TOP_K), gates, ys)


def _moe(tokens, router_w, router_b, w_gu, b_gu, w_down, b_down):
    n, d = tokens.shape
    top_idx, gates = _router(tokens, router_w, router_b)
    flat_e = top_idx.reshape(-1)
    onehot = (flat_e[:, None] == jnp.arange(N_EXPERTS, dtype=jnp.int32)[None, :]).astype(jnp.int32)
    rank = jnp.sum((jnp.cumsum(onehot, axis=0) - 1) * onehot, axis=1)
    counts = jnp.sum(onehot, axis=0)
    padded = (counts + MOE_BLOCK - 1) // MOE_BLOCK * MOE_BLOCK
    padded_end = jnp.cumsum(padded)
    padded_start = padded_end - padded
    dest = jnp.sum(onehot * padded_start[None, :], axis=1) + rank
    n_blocks = -(-(n * TOP_K + N_EXPERTS * (MOE_BLOCK - 1)) // MOE_BLOCK)
    cap = n_blocks * MOE_BLOCK
    n_used = (padded_end[-1] // MOE_BLOCK).astype(jnp.int32)
    blk = jnp.minimum(jnp.arange(n_blocks, dtype=jnp.int32), n_used - 1) * MOE_BLOCK
    block_e = jnp.minimum(jnp.sum((padded_end[None, :] <= blk[:, None]).astype(jnp.int32), axis=1),
                          N_EXPERTS - 1)
    xs = _dispatch(tokens, dest, cap)
    ys = _experts(xs, block_e, n_used.reshape(1), w_gu, b_gu, w_down, b_down)
    return _combine(ys, dest, gates)


def _rmsnorm(x, g):
    return x * lax.rsqrt(jnp.mean(x * x, axis=-1, keepdims=True) + NORM_EPS) * g


def _mm_tokens(h, w, b=None, **kw):
    lead = h.shape[:-1]
    return _mm(h.reshape(-1, h.shape[-1]), w, b, **kw).reshape(lead + (w.shape[1],))


def _shift_grid(h):
    bsz, t, d = h.shape
    g = h.reshape(bsz, t // GRID_W, GRID_W, d)
    c = d // 4
    left = jnp.pad(g[:, :, :-1, :c], ((0, 0), (0, 0), (1, 0), (0, 0)))
    right = jnp.pad(g[:, :, 1:, c:2 * c], ((0, 0), (0, 0), (0, 1), (0, 0)))
    up = jnp.pad(g[:, :-1, :, 2 * c:3 * c], ((0, 0), (1, 0), (0, 0), (0, 0)))
    down = jnp.pad(g[:, 1:, :, 3 * c:], ((0, 0), (0, 1), (0, 0), (0, 0)))
    return jnp.concatenate([left, right, up, down], axis=-1).reshape(bsz, t, d)


def _shift_seq(h):
    c = h.shape[-1] // 2
    prev = jnp.pad(h[:, :-1, :c], ((0, 0), (1, 0), (0, 0)))
    nxt = jnp.pad(h[:, 1:, c:], ((0, 0), (0, 1), (0, 0)))
    return jnp.concatenate([prev, nxt], axis=-1)


def _hgrn2_mixer(h, n_ctx, w_in, gnorm_w, w_out, lb):
    bsz, t, d = h.shape
    z = _mm_tokens(h.astype(BF16), w_in)
    o = _gla_scan(z, lb[0], n_ctx, False) + _gla_scan(z, lb[1], n_ctx, True)
    g = z[..., 4 * d:]
    oh = o.reshape(bsz, t, HG_HEADS, HG_DK)
    oh = _rmsnorm(oh, gnorm_w) * jax.nn.silu(g.reshape(bsz, t, HG_HEADS, HG_DK))
    return _mm_tokens(oh.reshape(bsz, t, d).astype(BF16), w_out)


def _rwkv7_mixer(h, n_ctx, mu, w_rkv, dec_w0, dec_w1, dec_w2, iclr_a0, iclr_a1, iclr_a2,
                 g1, g2, k_k, k_a, r_k, lnx_w, lnx_b, w_out):
    bsz, t, d = h.shape
    h_shift = jnp.concatenate([_shift_seq(h[:, :n_ctx]), _shift_grid(h[:, n_ctx:])], axis=1)
    dx = h_shift - h
    xr, xw, xk, xv, xa, xg = ((h + dx * mu[j]).astype(BF16) for j in range(6))
    heads = lambda a: a.reshape(bsz, t, RW_HEADS, RW_DK)
    r = _mm_tokens(xr, w_rkv[0])
    k = _mm_tokens(xk, w_rkv[1])
    v = _mm_tokens(xv, w_rkv[2])
    kk = heads(k * k_k)
    kk = (kk / jnp.maximum(jnp.sqrt(jnp.sum(kk * kk, axis=-1, keepdims=True)), 1e-12)).reshape(bsz, t, d)
    lora_w = jnp.tanh(_mm_tokens(xw, jnp.concatenate([dec_w1[0], dec_w1[1]], axis=1)))
    lora_a = _mm_tokens(xa, jnp.concatenate([iclr_a1[0], iclr_a1[1]], axis=1))
    nw = dec_w1.shape[-1]
    na = iclr_a1.shape[-1]
    y = 0.0
    bonus = 0.0
    for dd, rev in enumerate((False, True)):
        zw = _mm_tokens(lora_w[..., dd * nw:(dd + 1) * nw], dec_w2[dd], dec_w0[dd])
        lw = -jnp.exp(-jax.nn.softplus(-zw) - 0.5)
        a = jax.nn.sigmoid(_mm_tokens(lora_a[..., dd * na:(dd + 1) * na], iclr_a2[dd], iclr_a0[dd]))
        k_dir = k * (1.0 + (a - 1.0) * k_a)
        y = y + _rwkv_scan(lw, r, k_dir, v, kk, a, n_ctx, rev)
        bonus = bonus + jnp.sum(heads(r * k_dir) * r_k, axis=-1, keepdims=True)
    gate = _mm_tokens(jax.nn.sigmoid(_mm_tokens(xg, g1)), g2)
    yh = heads(y)
    mean = jnp.mean(yh, axis=-1, keepdims=True)
    var = jnp.mean(jnp.square(yh - mean), axis=-1, keepdims=True)
    yn = (yh - mean) * lax.rsqrt(var + RW_LNX_EPS)
    yn = yn * lnx_w.reshape(RW_HEADS, RW_DK) + lnx_b.reshape(RW_HEADS, RW_DK)
    yn = yn + bonus * heads(v)
    return _mm_tokens((yn.reshape(bsz, t, d) * gate).astype(BF16), w_out)


def kernel(x, c, ctx, c_ctx, ada_w, ada_b, norm_mix_g, norm_ffn_g, hg_w_in, hg_gnorm_w, hg_w_out, hg_lb, rw_mu, rw_w_rkv, rw_dec_w0, rw_dec_w1, rw_dec_w2, rw_iclr_a0, rw_iclr_a1, rw_iclr_a2, rw_g1, rw_g2, rw_k_k, rw_k_a, rw_r_k, rw_lnx_w, rw_lnx_b, rw_w_out, moe_router_w, moe_router_b, moe_w_gu, moe_b_gu, moe_w_down, moe_b_down, final_g):
    bsz, n_lat, d = x.shape
    n_ctx = ctx.shape[1]
    depth = ada_w.shape[0]
    lb_all = jnp.cumsum(jax.nn.softmax(hg_lb.astype(F32), axis=1), axis=1)
    cond = jnp.concatenate([jax.nn.silu(c), jax.nn.silu(c_ctx)[None, :],
                            jnp.zeros((8 - bsz - 1, d), F32)], axis=0)
    xs = jnp.concatenate([ctx, x], axis=1)
    t = n_ctx + n_lat
    for layer in range(depth):
        mod = _mm_exact(cond, ada_w[layer], ada_b[layer])
        mod_l = mod[:bsz].reshape(bsz, 1, 6, d)
        mod_c = jnp.broadcast_to(mod[bsz].reshape(1, 1, 6, d), (bsz, 1, 6, d))
        seg = lambda j: jnp.concatenate(
            [jnp.broadcast_to(mod_c[:, :, j], (bsz, n_ctx, d)),
             jnp.broadcast_to(mod_l[:, :, j], (bsz, n_lat, d))], axis=1)
        h = _rmsnorm(xs, norm_mix_g[layer]) * (1.0 + seg(1)) + seg(0)
        j = layer // 2
        if layer % 2 == 0:
            y = _hgrn2_mixer(h, n_ctx, hg_w_in[j], hg_gnorm_w[j], hg_w_out[j], lb_all[:, layer])
        else:
            y = _rwkv7_mixer(h, n_ctx, rw_mu[j], rw_w_rkv[j], rw_dec_w0[j], rw_dec_w1[j],
                             rw_dec_w2[j], rw_iclr_a0[j], rw_iclr_a1[j], rw_iclr_a2[j],
                             rw_g1[j], rw_g2[j], rw_k_k[j], rw_k_a[j], rw_r_k[j],
                             rw_lnx_w[j], rw_lnx_b[j], rw_w_out[j])
        xs = xs + seg(2) * y
        h = _rmsnorm(xs, norm_ffn_g[layer]) * (1.0 + seg(4)) + seg(3)
        out = _moe(h.reshape(bsz * t, d), moe_router_w[layer], moe_router_b[layer],
                   moe_w_gu[layer], moe_b_gu[layer], moe_w_down[layer], moe_b_down[layer])
        xs = xs + seg(5) * out.reshape(bsz, t, d)
    return _rmsnorm(xs[:, n_ctx:], final_g)
```

```python
import functools

import jax
import jax.numpy as jnp
from jax import lax
from jax.experimental import pallas as pl
from jax.experimental.pallas import tpu as pltpu

F32 = jnp.float32
BF16 = jnp.bfloat16
HIGHEST = lax.Precision.HIGHEST

GRID_W = 64
HG_HEADS = 8
HG_DK = 128
RW_HEADS = 16
RW_DK = 64
N_EXPERTS = 32
TOP_K = 4
SWIGLU_LIMIT = 7.0
SWIGLU_ALPHA = 1.702
NORM_EPS = 1e-6
RW_LNX_EPS = 1e-5 * RW_DK

LANES = 128
CHUNK = 64
SUB = 16
ROW_BLOCK = 256
MOE_BLOCK = 256
VMEM_LIMIT = 56 * 1024 * 1024

_NT = (((1,), (1,)), ((), ()))


def _dot(a, b):
    return jnp.dot(a.astype(BF16), b.astype(BF16), preferred_element_type=F32)


def _dot_nt(a, b):
    return lax.dot_general(a.astype(BF16), b.astype(BF16), _NT, preferred_element_type=F32)


def _split(a):
    hi = a.astype(BF16)
    lo = (a - hi.astype(F32)).astype(BF16)
    return hi, lo


def _dot3(a, b):
    a_hi, a_lo = _split(a)
    b_hi, b_lo = _split(b)
    f = lambda x, y: jnp.dot(x, y, preferred_element_type=F32)
    return f(a_hi, b_hi) + (f(a_hi, b_lo) + f(a_lo, b_hi))


def _head_ones(width):
    r = lax.broadcasted_iota(jnp.int32, (LANES, LANES), 0) // width
    c = lax.broadcasted_iota(jnp.int32, (LANES, LANES), 1) // width
    return (r == c).astype(BF16)


def _head_sums(x, ones_blk):
    hi, lo = _split(x)
    outs = []
    for c in range(x.shape[1] // LANES):
        sl = slice(c * LANES, (c + 1) * LANES)
        outs.append(jnp.dot(hi[:, sl], ones_blk, preferred_element_type=F32)
                    + jnp.dot(lo[:, sl], ones_blk, preferred_element_type=F32))
    return jnp.concatenate(outs, axis=1)


def _norm_modulate(x, g, shift, scale):
    y = x * lax.rsqrt(jnp.mean(x * x, axis=-1, keepdims=True) + NORM_EPS) * g
    return y * (1.0 + scale) + shift


def _params(sem):
    return pltpu.CompilerParams(dimension_semantics=sem, vmem_limit_bytes=VMEM_LIMIT)


class _Layout:
    def __init__(self, bsz, n_lat, n_ctx):
        assert n_lat % ROW_BLOCK == 0 and n_ctx % ROW_BLOCK == 0
        self.bsz, self.n_lat, self.n_ctx = bsz, n_lat, n_ctx
        self.nlb = n_lat // ROW_BLOCK
        self.ncb = n_ctx // ROW_BLOCK
        self.rows = bsz * (n_lat + n_ctx)

    def segment(self, i):
        return jnp.minimum(i // self.nlb, self.bsz)

    def seq_block(self, b, c, rev):
        if rev:
            s = jnp.where(c < self.ncb, self.ncb - 1 - c, self.nlb + self.ncb - 1 - (c - self.ncb))
        else:
            s = c
        return jnp.where(s < self.ncb, self.bsz * self.nlb + b * self.ncb + s,
                         b * self.nlb + (s - self.ncb))


def _mm_body(x_ref, w_ref, b_ref, o_ref, wb_ref, *, act):
    @pl.when(pl.program_id(1) == 0)
    def _():
        wb_ref[...] = w_ref[...].astype(BF16)

    acc = jnp.dot(x_ref[...].astype(BF16), wb_ref[...], preferred_element_type=F32) + b_ref[...]
    if act == "tanh":
        acc = jnp.tanh(acc)
    elif act == "sigmoid":
        acc = jax.nn.sigmoid(acc)
    o_ref[...] = acc.astype(o_ref.dtype)


def _mm(x, w, b=None, act=None, out_dtype=F32, tm=512, tn=512):
    m, k = x.shape
    n = w.shape[1]
    tm = min(tm, m)
    tn = min(tn, n)
    assert m % tm == 0 and n % tn == 0, (m, n, tm, tn)
    if b is None:
        b = jnp.zeros((n,), F32)
    return pl.pallas_call(
        functools.partial(_mm_body, act=act),
        grid=(n // tn, m // tm),
        in_specs=[
            pl.BlockSpec((tm, k), lambda j, i: (i, 0)),
            pl.BlockSpec((k, tn), lambda j, i: (0, j)),
            pl.BlockSpec((1, tn), lambda j, i: (0, j)),
        ],
        out_specs=pl.BlockSpec((tm, tn), lambda j, i: (i, j)),
        out_shape=jax.ShapeDtypeStruct((m, n), out_dtype),
        scratch_shapes=[pltpu.VMEM((k, tn), BF16)],
        compiler_params=_params(("arbitrary", "arbitrary")),
        name="mm",
    )(x, w, b.reshape(1, n).astype(F32))


def _mm_exact_body(x_ref, w_ref, b_ref, o_ref):
    acc = jnp.dot(x_ref[...], w_ref[...], precision=HIGHEST, preferred_element_type=F32)
    o_ref[...] = acc + b_ref[...]


def _mm_exact(x, w, b, tn=512):
    m, k = x.shape
    n = w.shape[1]
    tn = min(tn, n)
    return pl.pallas_call(
        _mm_exact_body,
        grid=(n // tn,),
        in_specs=[
            pl.BlockSpec((m, k), lambda j: (0, 0)),
            pl.BlockSpec((k, tn), lambda j: (0, j)),
            pl.BlockSpec((1, tn), lambda j: (0, j)),
        ],
        out_specs=pl.BlockSpec((m, tn), lambda j: (0, j)),
        out_shape=jax.ShapeDtypeStruct((m, n), F32),
        compiler_params=_params(("arbitrary",)),
        name="mm_exact",
    )(x, w, b.reshape(1, n))


def _norm_mod_body(x_ref, g_ref, m_ref, o_ref):
    o_ref[...] = _norm_modulate(x_ref[...], g_ref[...], m_ref[0, 0:1], m_ref[0, 1:2]).astype(o_ref.dtype)


def _norm_mod(xs, g, modt, lay, out_dtype):
    m, d = xs.shape
    return pl.pallas_call(
        _norm_mod_body,
        grid=(m // ROW_BLOCK,),
        in_specs=[
            pl.BlockSpec((ROW_BLOCK, d), lambda i: (i, 0)),
            pl.BlockSpec((1, d), lambda i: (0, 0)),
            pl.BlockSpec((1,) + modt.shape[1:], lambda i: (lay.segment(i), 0, 0)),
        ],
        out_specs=pl.BlockSpec((ROW_BLOCK, d), lambda i: (i, 0)),
        out_shape=jax.ShapeDtypeStruct((m, d), out_dtype),
        compiler_params=_params(("arbitrary",)),
        name="norm_mod",
    )(xs, g.reshape(1, d), modt)


HG_HB = 4


def _gla_chunk(q, f, v, lb, sts, rev, consts):
    tri, ones_kk, trow = consts
    heads = range(len(sts))
    hs = [slice(h * HG_DK, (h + 1) * HG_DK) for h in heads]
    q = q * jax.nn.sigmoid(q)
    g = jnp.log(lb + (1.0 - lb) * jax.nn.sigmoid(f))
    k = (1.0 - lb) * jax.nn.sigmoid(-f)
    b = jnp.dot(tri, g, precision=HIGHEST, preferred_element_type=F32)
    b_end = b[0:1] if rev else b[CHUNK - 1:CHUNK]
    q_in = q * jnp.exp(b)
    k_end = k * jnp.exp(b_end - b)
    o_inter = [_dot_nt(q_in[:, s], sts[h]) for h, s in zip(heads, hs)]
    kv = [_dot(v[:, s].T, k_end[:, s]) for s in hs]
    e_end = jnp.exp(b_end)
    sts_new = [sts[h] * e_end[:, hs[h]] + kv[h] for h in heads]

    outs = [None] * (CHUNK // SUB)
    for p in range(CHUNK // SUB):
        if rev:
            r0 = CHUNK - SUB * (p + 1)
            e0, e1, mrow = CHUNK - SUB * p, CHUNK, CHUNK - SUB * p
        else:
            r0 = SUB * p
            e0, e1, mrow = 0, SUB * p, SUB * p - 1
        qp, bp, kp, vp = (x[r0:r0 + SUB] for x in (q, b, k, v))
        acc = [o_inter[h][r0:r0 + SUB] for h in heads]
        if p > 0:
            m = b[mrow:mrow + 1]
            q_lo = qp * jnp.exp(bp - m)
            k_hi = k[e0:e1] * jnp.exp(m - b[e0:e1])
            sc = [_dot_nt(q_lo[:, s], k_hi[:, s]) for s in hs]
            acc = [acc[h] + _dot(sc[h], v[e0:e1, hs[h]]) for h in heads]
        terms = []
        for s in range(SUB):
            mask = (trow <= s) if rev else (trow >= s)
            w = jnp.exp(jnp.where(mask, bp - bp[s:s + 1], -1e30))
            terms.append((qp * kp[s:s + 1] * w).astype(BF16))
        terms = jnp.concatenate(terms, axis=0)
        sums = [jnp.dot(terms[:, s], ones_kk, preferred_element_type=F32) for s in hs]
        for s in range(SUB):
            acc = [acc[h] + sums[h][s * SUB:(s + 1) * SUB] * vp[s:s + 1, hs[h]] for h in heads]
        outs[r0 // SUB] = jnp.concatenate(acc, axis=1)
    return jnp.concatenate(outs, axis=0), sts_new


def _gla_body(q_ref, f_ref, v_ref, lb_ref, o_ref, st_ref, *, rev):
    @pl.when(pl.program_id(2) == 0)
    def _():
        st_ref[...] = jnp.zeros_like(st_ref)

    row = lax.broadcasted_iota(jnp.int32, (CHUNK, CHUNK), 0)
    col = lax.broadcasted_iota(jnp.int32, (CHUNK, CHUNK), 1)
    tri = ((row <= col) if rev else (row >= col)).astype(F32)
    ones_kk = jnp.ones((HG_DK, HG_DK), BF16)
    trow = lax.broadcasted_iota(jnp.int32, (SUB, HG_HB * HG_DK), 0)
    lb = lb_ref[0]
    n_chunks = ROW_BLOCK // CHUNK

    def step(i, carry):
        cc = (n_chunks - 1 - i) if rev else i
        sl = pl.ds(pl.multiple_of(cc * CHUNK, CHUNK), CHUNK)
        o, sts = _gla_chunk(q_ref[sl, :], f_ref[sl, :], v_ref[sl, :], lb,
                            [st_ref[h] for h in range(HG_HB)], rev, (tri, ones_kk, trow))
        o_ref[sl, :] = o
        for h in range(HG_HB):
            st_ref[h] = sts[h]
        return carry

    lax.fori_loop(0, n_chunks, step, 0)


def _gla_scan(z, lb, lay, rev):
    m = z.shape[0]
    d_model = HG_HEADS * HG_DK
    ng = HG_HEADS // HG_HB
    width = HG_HB * HG_DK
    f_off = ng * (2 if rev else 1)
    v_off = ng * 3
    rmap = lambda b, c: lay.seq_block(b, c, rev)
    return pl.pallas_call(
        functools.partial(_gla_body, rev=rev),
        grid=(lay.bsz, ng, lay.nlb + lay.ncb),
        in_specs=[
            pl.BlockSpec((ROW_BLOCK, width), lambda b, h, c: (rmap(b, c), h)),
            pl.BlockSpec((ROW_BLOCK, width), lambda b, h, c: (rmap(b, c), f_off + h)),
            pl.BlockSpec((ROW_BLOCK, width), lambda b, h, c: (rmap(b, c), v_off + h)),
            pl.BlockSpec((1, 1, width), lambda b, h, c: (h, 0, 0)),
        ],
        out_specs=pl.BlockSpec((ROW_BLOCK, width), lambda b, h, c: (rmap(b, c), h)),
        out_shape=jax.ShapeDtypeStruct((m, d_model), F32),
        scratch_shapes=[pltpu.VMEM((HG_HB, HG_DK, HG_DK), F32)],
        compiler_params=_params(("arbitrary", "arbitrary", "arbitrary")),
        name="gla_rev" if rev else "gla_fwd",
    )(z, z, z, lb.reshape(ng, 1, width))


def _hg_readout_body(of_ref, ob_ref, g_ref, gw_ref, w_ref, xs_ref, m_ref, ng_ref,
                     xo_ref, ho_ref, wb_ref):
    @pl.when(pl.program_id(0) == 0)
    def _():
        wb_ref[...] = w_ref[...].astype(BF16)

    o = of_ref[...] + ob_ref[...]
    parts = []
    for h in range(HG_HEADS):
        blk = o[:, h * HG_DK:(h + 1) * HG_DK]
        ms = jnp.mean(blk * blk, axis=-1, keepdims=True)
        parts.append(blk * lax.rsqrt(ms + NORM_EPS))
    gate = g_ref[...]
    y = jnp.concatenate(parts, axis=1) * gw_ref[...] * (gate * jax.nn.sigmoid(gate))
    acc = jnp.dot(y.astype(BF16), wb_ref[...], preferred_element_type=F32)
    xs_new = xs_ref[...] + m_ref[0, 2:3] * acc
    xo_ref[...] = xs_new
    ho_ref[...] = _norm_modulate(xs_new, ng_ref[...], m_ref[0, 3:4], m_ref[0, 4:5])


def _hg_readout(o_f, o_b, z, gnorm_w, w_out, xs, modt, norm_g, lay):
    m, d = xs.shape
    row = pl.BlockSpec((ROW_BLOCK, d), lambda i: (i, 0))
    vec = pl.BlockSpec((1, d), lambda i: (0, 0))
    return pl.pallas_call(
        _hg_readout_body,
        grid=(m // ROW_BLOCK,),
        in_specs=[row, row,
                  pl.BlockSpec((ROW_BLOCK, d), lambda i: (i, 4)),
                  vec,
                  pl.BlockSpec((d, d), lambda i: (0, 0)),
                  row,
                  pl.BlockSpec((1,) + modt.shape[1:], lambda i: (lay.segment(i), 0, 0)),
                  vec],
        out_specs=[row, row],
        out_shape=[jax.ShapeDtypeStruct((m, d), F32), jax.ShapeDtypeStruct((m, d), F32)],
        scratch_shapes=[pltpu.VMEM((d, d), BF16)],
        compiler_params=_params(("arbitrary",)),
        name="hg_readout",
    )(o_f, o_b, z, jnp.tile(gnorm_w, HG_HEADS).reshape(1, d), w_out, xs, modt,
      norm_g.reshape(1, d))


RW_HB = 8


def _rwkv_chunk(lw, cum, r, k, v, kap, a, sts, rev, consts):
    incl, strict, eye = consts
    nh = len(sts)
    heads = range(nh)
    hs = [slice(h * RW_DK, (h + 1) * RW_DK) for h in heads]
    cum_ex = cum - lw
    tot = cum[0:1] if rev else cum[CHUNK - 1:CHUNK]
    e_neg = jnp.exp(-cum)
    kt = kap * jnp.exp(cum_ex)
    bb = kap * a
    bt = bb * e_neg
    kd = k * e_neg
    rt = r * jnp.exp(cum)
    dec_end = jnp.exp(tot - cum)
    b_end = bb * dec_end
    k_end = k * dec_end
    e_tot = jnp.exp(tot)

    gram = [_dot_nt(jnp.concatenate([kt[:, s], rt[:, s]], axis=0),
                    jnp.concatenate([bt[:, s], kd[:, s]], axis=0)) for s in hs]
    a_bk = [jnp.where(strict, g[:CHUNK, :CHUNK], 0.0) for g in gram]
    a_kr = [jnp.concatenate([jnp.where(strict, g[:CHUNK, CHUNK:], 0.0),
                             jnp.where(incl, g[CHUNK:, CHUNK:], 0.0)], axis=0) for g in gram]
    a_rb = [jnp.where(incl, g[CHUNK:, :CHUNK], 0.0) for g in gram]
    av = [_dot(a_kr[h], v[:, hs[h]]) for h in heads]

    tinv = [eye - x for x in a_bk]
    pw = [_dot(x, x) for x in a_bk]
    for _ in range(4):
        tp = [_dot(jnp.concatenate([tinv[h], pw[h]], axis=0), pw[h]) for h in heads]
        tinv = [tinv[h] + tp[h][:CHUNK] for h in heads]
        pw = [x[CHUNK:] for x in tp]
    tinv = [tinv[h] + _dot(tinv[h], pw[h]) for h in heads]

    ku = [_dot(tinv[h], jnp.concatenate([kt[:, hs[h]], av[h][:CHUNK]], axis=1)) for h in heads]
    ab = [_dot(a_rb[h], ku[h]) for h in heads]
    rh = [rt[:, hs[h]] - ab[h][:, :RW_DK] for h in heads]
    y0 = [av[h][CHUNK:] - ab[h][:, RW_DK:] for h in heads]
    ys = [_dot_nt(rh[h], sts[h]) + y0[h] for h in heads]
    mb = [_dot(ku[h].T, b_end[:, hs[h]]) for h in heads]
    m_mat = [jnp.where(eye > 0, e_tot[:, hs[h]], 0.0) - mb[h][:RW_DK] for h in heads]
    n_mat = [_dot(v[:, hs[h]].T, k_end[:, hs[h]]) - mb[h][RW_DK:] for h in heads]
    sts_new = [_dot3(sts[h], m_mat[h]) + n_mat[h] for h in heads]
    return jnp.concatenate(ys, axis=1), sts_new


def _rwkv_body(r_ref, k_ref, v_ref, zw_ref, za_ref, kk_ref, ka_ref, rk_ref,
               y_ref, bv_ref, st_ref, *, rev):
    @pl.when(pl.program_id(2) == 0)
    def _():
        st_ref[...] = jnp.zeros_like(st_ref)

    row = lax.broadcasted_iota(jnp.int32, (CHUNK, CHUNK), 0)
    col = lax.broadcasted_iota(jnp.int32, (CHUNK, CHUNK), 1)
    incl = (row <= col) if rev else (row >= col)
    strict = (row < col) if rev else (row > col)
    eye = (row == col).astype(F32)
    tri = incl.astype(F32)
    ones_blk = _head_ones(RW_DK)
    n_chunks = ROW_BLOCK // CHUNK

    def step(i, carry):
        cc = (n_chunks - 1 - i) if rev else i
        sl = pl.ds(pl.multiple_of(cc * CHUNK, CHUNK), CHUNK)
        r, k, v, zw, za = (ref[sl, :] for ref in (r_ref, k_ref, v_ref, zw_ref, za_ref))
        kmul = k * kk_ref[...]
        kap = kmul * lax.rsqrt(jnp.maximum(_head_sums(kmul * kmul, ones_blk), 1e-24))
        a = jax.nn.sigmoid(za)
        lw = -jnp.exp(-jax.nn.softplus(-zw) - 0.5)
        k_dir = k * (1.0 + (a - 1.0) * ka_ref[...])
        cum = jnp.dot(tri, lw, precision=HIGHEST, preferred_element_type=F32)
        y, sts = _rwkv_chunk(lw, cum, r, k_dir, v, kap, a, [st_ref[h] for h in range(RW_HB)],
                             rev, (incl, strict, eye))
        for h in range(RW_HB):
            st_ref[h] = sts[h]
        y_ref[sl, :] = y
        bv_ref[sl, :] = _head_sums(r * k_dir * rk_ref[...], ones_blk) * v
        return carry

    lax.fori_loop(0, n_chunks, step, 0)


def _rwkv_scan(r, k, v, zw, za, k_k, k_a, r_k, lay, rev):
    m, d_model = r.shape
    width = RW_HB * RW_DK
    rmap = lambda b, c: lay.seq_block(b, c, rev)
    spec = pl.BlockSpec((ROW_BLOCK, width), lambda b, h, c: (rmap(b, c), h))
    vec = pl.BlockSpec((1, width), lambda b, h, c: (0, h))
    out = jax.ShapeDtypeStruct((m, d_model), F32)
    return pl.pallas_call(
        functools.partial(_rwkv_body, rev=rev),
        grid=(lay.bsz, d_model // width, lay.nlb + lay.ncb),
        in_specs=[spec] * 5 + [vec] * 3,
        out_specs=[spec, spec],
        out_shape=[out, out],
        scratch_shapes=[pltpu.VMEM((RW_HB, RW_DK, RW_DK), F32)],
        compiler_params=_params(("arbitrary", "arbitrary", "arbitrary")),
        name="rwkv_rev" if rev else "rwkv_fwd",
    )(r, k, v, zw, za, k_k.reshape(1, d_model), k_a.reshape(1, d_model), r_k.reshape(1, d_model))


def _rw_readout_body(yf_ref, yb_ref, bf_ref, bb_ref, g_ref, lw_ref, lb_ref, w_ref, xs_ref,
                     m_ref, ng_ref, xo_ref, ho_ref, wb_ref):
    @pl.when(pl.program_id(0) == 0)
    def _():
        wb_ref[...] = w_ref[...].astype(BF16)

    ones_blk = _head_ones(RW_DK)
    y = yf_ref[...] + yb_ref[...]
    dev = y - _head_sums(y, ones_blk) * (1.0 / RW_DK)
    var = _head_sums(dev * dev, ones_blk) * (1.0 / RW_DK)
    yn = dev * lax.rsqrt(var + RW_LNX_EPS) * lw_ref[...] + lb_ref[...]
    yn = yn + bf_ref[...] + bb_ref[...]
    acc = jnp.dot((yn * g_ref[...]).astype(BF16), wb_ref[...], preferred_element_type=F32)
    xs_new = xs_ref[...] + m_ref[0, 2:3] * acc
    xo_ref[...] = xs_new
    ho_ref[...] = _norm_modulate(xs_new, ng_ref[...], m_ref[0, 3:4], m_ref[0, 4:5])


def _rw_readout(y_f, y_b, bv_f, bv_b, gate, lnx_w, lnx_b, w_out, xs, modt, norm_g, lay):
    m, d = xs.shape
    row = pl.BlockSpec((ROW_BLOCK, d), lambda i: (i, 0))
    vec = pl.BlockSpec((1, d), lambda i: (0, 0))
    return pl.pallas_call(
        _rw_readout_body,
        grid=(m // ROW_BLOCK,),
        in_specs=[row] * 5 + [vec, vec, pl.BlockSpec((d, d), lambda i: (0, 0)), row,
                              pl.BlockSpec((1,) + modt.shape[1:],
                                           lambda i: (lay.segment(i), 0, 0)),
                              vec],
        out_specs=[row, row],
        out_shape=[jax.ShapeDtypeStruct((m, d), F32), jax.ShapeDtypeStruct((m, d), F32)],
        scratch_shapes=[pltpu.VMEM((d, d), BF16)],
        compiler_params=_params(("arbitrary",)),
        name="rw_readout",
    )(y_f, y_b, bv_f, bv_b, gate, lnx_w.reshape(1, d), lnx_b.reshape(1, d), w_out, xs, modt,
      norm_g.reshape(1, d))


def _router_body(x_ref, w_ref, b_ref, idx_ref, gate_ref):
    logits = jnp.dot(x_ref[...], w_ref[...], precision=HIGHEST,
                     preferred_element_type=F32) + b_ref[...]
    lane = lax.broadcasted_iota(jnp.int32, logits.shape, 1)
    vals, idxs = [], []
    for _ in range(TOP_K):
        m = jnp.max(logits, axis=-1, keepdims=True)
        idx = jnp.min(jnp.where(logits == m, lane, N_EXPERTS), axis=-1, keepdims=True)
        vals.append(m)
        idxs.append(idx)
        logits = jnp.where(lane == idx, -jnp.inf, logits)
    exps = [jnp.exp(val - vals[0]) for val in vals]
    denom = exps[0] + exps[1] + exps[2] + exps[3]
    idx_ref[...] = jnp.concatenate(idxs, axis=1)
    gate_ref[...] = jnp.concatenate(exps, axis=1) / denom


def _router(x, n, w, b):
    d = x.shape[1]
    return pl.pallas_call(
        _router_body,
        grid=(n // ROW_BLOCK,),
        in_specs=[
            pl.BlockSpec((ROW_BLOCK, d), lambda i: (i, 0)),
            pl.BlockSpec((d, N_EXPERTS), lambda i: (0, 0)),
            pl.BlockSpec((1, N_EXPERTS), lambda i: (0, 0)),
        ],
        out_specs=[pl.BlockSpec((ROW_BLOCK, TOP_K), lambda i: (i, 0)),
                   pl.BlockSpec((ROW_BLOCK, TOP_K), lambda i: (i, 0))],
        out_shape=[jax.ShapeDtypeStruct((n, TOP_K), jnp.int32),
                   jax.ShapeDtypeStruct((n, TOP_K), F32)],
        compiler_params=_params(("arbitrary",)),
        name="router",
    )(x, w, b.reshape(1, N_EXPERTS))


EXPERT_COLS = 512


def _expert_body(be_ref, nu_ref, x_ref, wgu_ref, bgu_ref, wd_ref, bd_ref, o_ref,
                 wgu_b, wd_b):
    i = pl.program_id(0)
    d_exp = wd_ref.shape[1]
    prev = be_ref[jnp.maximum(i - 1, 0)]

    @pl.when((i == 0) | (be_ref[i] != prev))
    def _():
        wgu_b[...] = wgu_ref[0].astype(BF16)
        wd_b[...] = wd_ref[0].astype(BF16)

    @pl.when(i < nu_ref[0])
    def _():
        x = x_ref[...].astype(BF16)
        acc = jnp.zeros(o_ref.shape, F32)
        for c0 in range(0, d_exp, EXPERT_COLS):
            gate = jnp.dot(x, wgu_b[:, c0:c0 + EXPERT_COLS], preferred_element_type=F32)
            gate = gate + bgu_ref[0, :, c0:c0 + EXPERT_COLS]
            up = jnp.dot(x, wgu_b[:, d_exp + c0:d_exp + c0 + EXPERT_COLS],
                         preferred_element_type=F32)
            up = up + bgu_ref[0, :, d_exp + c0:d_exp + c0 + EXPERT_COLS]
            gate = jnp.minimum(gate, SWIGLU_LIMIT)
            up = jnp.clip(up, -SWIGLU_LIMIT, SWIGLU_LIMIT)
            act = gate * jax.nn.sigmoid(SWIGLU_ALPHA * gate) * (up + 1.0)
            acc = acc + jnp.dot(act.astype(BF16), wd_b[c0:c0 + EXPERT_COLS, :],
                                preferred_element_type=F32)
        o_ref[...] = acc + bd_ref[0]

    @pl.when(i >= nu_ref[0])
    def _():
        o_ref[...] = jnp.zeros_like(o_ref)


def _experts(xs, block_e, n_used, w_gu, b_gu, w_down, b_down):
    cap, d = xs.shape
    n_e, _, d2 = w_gu.shape
    d_exp = d2 // 2
    nb = cap // MOE_BLOCK
    grid_spec = pltpu.PrefetchScalarGridSpec(
        num_scalar_prefetch=2,
        grid=(nb,),
        in_specs=[
            pl.BlockSpec((MOE_BLOCK, d), lambda i, be, nu: (i, 0)),
            pl.BlockSpec((1, d, d2), lambda i, be, nu: (be[i], 0, 0)),
            pl.BlockSpec((1, 1, d2), lambda i, be, nu: (be[i], 0, 0)),
            pl.BlockSpec((1, d_exp, d), lambda i, be, nu: (be[i], 0, 0)),
            pl.BlockSpec((1, 1, d), lambda i, be, nu: (be[i], 0, 0)),
        ],
        out_specs=pl.BlockSpec((MOE_BLOCK, d), lambda i, be, nu: (i, 0)),
        scratch_shapes=[pltpu.VMEM((d, d2), BF16), pltpu.VMEM((d_exp, d), BF16)],
    )
    return pl.pallas_call(
        _expert_body,
        grid_spec=grid_spec,
        out_shape=jax.ShapeDtypeStruct((cap, d), F32),
        compiler_params=_params(("arbitrary",)),
        name="experts",
    )(block_e, n_used, xs, w_gu, b_gu.reshape(n_e, 1, d2), w_down, b_down.reshape(n_e, 1, d))


def _row_copies(make_copy, start):
    def body(r, carry):
        for j in range(TOP_K):
            cp = make_copy(r, j)
            cp.start(priority=j % 2) if start else cp.wait()
        return carry

    lax.fori_loop(0, ROW_BLOCK, body, 0, unroll=8)


def _dispatch_body(dst_ref, x_ref, init_ref, xs_ref, sem):
    del init_ref

    def copy(r, j):
        return pltpu.make_async_copy(x_ref.at[pl.ds(r, 1)],
                                     xs_ref.at[pl.ds(dst_ref[0, 0, r * TOP_K + j], 1)], sem)

    _row_copies(copy, True)
    _row_copies(copy, False)


def _dispatch(tokens, n, dest, cap):
    d = tokens.shape[1]
    nblk = n // ROW_BLOCK
    return pl.pallas_call(
        _dispatch_body,
        grid=(nblk,),
        in_specs=[
            pl.BlockSpec((1, 1, ROW_BLOCK * TOP_K), lambda i: (i, 0, 0),
                         memory_space=pltpu.SMEM),
            pl.BlockSpec((ROW_BLOCK, d), lambda i: (i, 0)),
            pl.BlockSpec(memory_space=pl.ANY),
        ],
        out_specs=pl.BlockSpec(memory_space=pl.ANY),
        out_shape=jax.ShapeDtypeStruct((cap, d), F32),
        scratch_shapes=[pltpu.SemaphoreType.DMA(())],
        input_output_aliases={2: 0},
        compiler_params=_params(("arbitrary",)),
        name="dispatch",
    )(dest.reshape(nblk, 1, ROW_BLOCK * TOP_K), tokens, jnp.zeros((cap, d), F32))


def _combine_body(dst_ref, gate_ref, xs_ref, m_ref, ng_ref, m2_ref, ys_ref, xo_ref, ho_ref,
                  buf, sem):
    def copy(r, j):
        return pltpu.make_async_copy(ys_ref.at[pl.ds(dst_ref[0, 0, r * TOP_K + j], 1)],
                                     buf.at[j, pl.ds(r, 1)], sem)

    _row_copies(copy, True)
    _row_copies(copy, False)
    gates = gate_ref[...]
    acc = gates[:, 0:1] * buf[0]
    for j in range(1, TOP_K):
        acc = acc + gates[:, j:j + 1] * buf[j]
    xs_new = xs_ref[...] + m_ref[0, 5:6] * acc
    xo_ref[...] = xs_new
    ho_ref[...] = _norm_modulate(xs_new, ng_ref[...], m2_ref[0, 0:1], m2_ref[0, 1:2])


def _combine(ys, dest, gates, xs, modt, next_g, next_modt, lay):
    n = gates.shape[0]
    d = ys.shape[1]
    nblk = n // ROW_BLOCK
    row = pl.BlockSpec((ROW_BLOCK, d), lambda i: (i, 0))
    seg = lambda mt: pl.BlockSpec((1,) + mt.shape[1:], lambda i: (lay.segment(i), 0, 0))
    return pl.pallas_call(
        _combine_body,
        grid=(nblk,),
        in_specs=[
            pl.BlockSpec((1, 1, ROW_BLOCK * TOP_K), lambda i: (i, 0, 0),
                         memory_space=pltpu.SMEM),
            pl.BlockSpec((ROW_BLOCK, TOP_K), lambda i: (i, 0)),
            row, seg(modt), pl.BlockSpec((1, d), lambda i: (0, 0)), seg(next_modt),
            pl.BlockSpec(memory_space=pl.ANY),
        ],
        out_specs=[row, row],
        out_shape=[jax.ShapeDtypeStruct((n, d), F32), jax.ShapeDtypeStruct((n, d), F32)],
        scratch_shapes=[pltpu.VMEM((TOP_K, ROW_BLOCK, d), F32), pltpu.SemaphoreType.DMA(())],
        compiler_params=_params(("arbitrary",)),
        name="combine",
    )(dest.reshape(nblk, 1, ROW_BLOCK * TOP_K), gates, xs, modt, next_g.reshape(1, d),
      next_modt, ys)


def _moe(h, n, xs, modt, next_g, next_modt, lay, router_w, router_b, w_gu, b_gu, w_down, b_down):
    d = h.shape[1]
    top_idx, gates = _router(h, n, router_w, router_b)
    flat_e = top_idx.reshape(-1)
    onehot = (flat_e[:, None] == jnp.arange(N_EXPERTS, dtype=jnp.int32)[None, :]).astype(jnp.int32)
    rank = jnp.sum((jnp.cumsum(onehot, axis=0) - 1) * onehot, axis=1)
    counts = jnp.sum(onehot, axis=0)
    padded = (counts + MOE_BLOCK - 1) // MOE_BLOCK * MOE_BLOCK
    padded_end = jnp.cumsum(padded)
    padded_start = padded_end - padded
    dest = jnp.sum(onehot * padded_start[None, :], axis=1) + rank
    n_blocks = -(-(n * TOP_K + N_EXPERTS * (MOE_BLOCK - 1)) // MOE_BLOCK)
    cap = n_blocks * MOE_BLOCK
    n_used = (padded_end[-1] // MOE_BLOCK).astype(jnp.int32)
    blk = jnp.minimum(jnp.arange(n_blocks, dtype=jnp.int32), n_used - 1) * MOE_BLOCK
    block_e = jnp.minimum(jnp.sum((padded_end[None, :] <= blk[:, None]).astype(jnp.int32), axis=1),
                          N_EXPERTS - 1)
    slots = _dispatch(h, n, dest, cap)
    ys = _experts(slots, block_e, n_used.reshape(1), w_gu, b_gu, w_down, b_down)
    return _combine(ys, dest, gates, xs, modt, next_g, next_modt, lay)


def _shift_grid(h):
    bsz, t, d = h.shape
    g = h.reshape(bsz, t // GRID_W, GRID_W, d)
    c = d // 4
    left = jnp.pad(g[:, :, :-1, :c], ((0, 0), (0, 0), (1, 0), (0, 0)))
    right = jnp.pad(g[:, :, 1:, c:2 * c], ((0, 0), (0, 0), (0, 1), (0, 0)))
    up = jnp.pad(g[:, :-1, :, 2 * c:3 * c], ((0, 0), (1, 0), (0, 0), (0, 0)))
    down = jnp.pad(g[:, 1:, :, 3 * c:], ((0, 0), (0, 1), (0, 0), (0, 0)))
    return jnp.concatenate([left, right, up, down], axis=-1).reshape(bsz, t, d)


def _shift_seq(h):
    c = h.shape[-1] // 2
    prev = jnp.pad(h[:, :-1, :c], ((0, 0), (1, 0), (0, 0)))
    nxt = jnp.pad(h[:, 1:, c:], ((0, 0), (0, 1), (0, 0)))
    return jnp.concatenate([prev, nxt], axis=-1)


def _rwkv7_mixer(h, xs, modt, norm_g, lay, mu, w_rkv, dec_w0, dec_w1, dec_w2, iclr_a0, iclr_a1,
                 iclr_a2, g1, g2, k_k, k_a, r_k, lnx_w, lnx_b, w_out):
    m, d = h.shape
    n_l = lay.bsz * lay.n_lat
    h_shift = jnp.concatenate(
        [_shift_grid(h[:n_l].reshape(lay.bsz, lay.n_lat, d)).reshape(n_l, d),
         _shift_seq(h[n_l:].reshape(lay.bsz, lay.n_ctx, d)).reshape(m - n_l, d)], axis=0)
    dx = h_shift - h
    xr, xw, xk, xv, xa, xg = ((h + dx * mu[j]).astype(BF16) for j in range(6))
    r = _mm(xr, w_rkv[0])
    k = _mm(xk, w_rkv[1])
    v = _mm(xv, w_rkv[2])
    lora_w = _mm(xw, jnp.concatenate([dec_w1[0], dec_w1[1]], axis=1), act="tanh", out_dtype=BF16)
    lora_a = _mm(xa, jnp.concatenate([iclr_a1[0], iclr_a1[1]], axis=1), out_dtype=BF16)
    gate = _mm(_mm(xg, g1, act="sigmoid", out_dtype=BF16), g2)
    nw = dec_w1.shape[-1]
    na = iclr_a1.shape[-1]
    ys = []
    for dd, rev in enumerate((False, True)):
        zw = _mm(lora_w[:, dd * nw:(dd + 1) * nw], dec_w2[dd], dec_w0[dd])
        za = _mm(lora_a[:, dd * na:(dd + 1) * na], iclr_a2[dd], iclr_a0[dd])
        ys.append(_rwkv_scan(r, k, v, zw, za, k_k, k_a, r_k, lay, rev))
    return _rw_readout(ys[0][0], ys[1][0], ys[0][1], ys[1][1], gate, lnx_w, lnx_b, w_out,
                       xs, modt, norm_g, lay)


def kernel(x, c, ctx, c_ctx, ada_w, ada_b, norm_mix_g, norm_ffn_g, hg_w_in, hg_gnorm_w, hg_w_out, hg_lb, rw_mu, rw_w_rkv, rw_dec_w0, rw_dec_w1, rw_dec_w2, rw_iclr_a0, rw_iclr_a1, rw_iclr_a2, rw_g1, rw_g2, rw_k_k, rw_k_a, rw_r_k, rw_lnx_w, rw_lnx_b, rw_w_out, moe_router_w, moe_router_b, moe_w_gu, moe_b_gu, moe_w_down, moe_b_down, final_g):
    bsz, n_lat, d = x.shape
    n_ctx = ctx.shape[1]
    depth = ada_w.shape[0]
    lay = _Layout(bsz, n_lat, n_ctx)
    lb_all = jnp.cumsum(jax.nn.softmax(hg_lb.astype(F32), axis=1), axis=1)
    cond = jnp.concatenate([jax.nn.silu(c), jax.nn.silu(c_ctx)[None, :],
                            jnp.zeros((8 - bsz - 1, d), F32)], axis=0)
    modts = [_mm_exact(cond, ada_w[l], ada_b[l])[:bsz + 1].reshape(bsz + 1, 6, d)
             for l in range(depth)]
    xs = jnp.concatenate([x.reshape(bsz * n_lat, d), ctx.reshape(bsz * n_ctx, d)], axis=0)
    h = _norm_mod(xs, norm_mix_g[0], modts[0], lay, BF16)
    n_rows = lay.rows
    for layer in range(depth):
        last = layer == depth - 1
        modt = modts[layer]
        j = layer // 2
        if layer % 2 == 0:
            z = _mm(h.astype(BF16), hg_w_in[j])
            lb = lb_all[:, layer]
            o_f = _gla_scan(z, lb[0], lay, False)
            o_b = _gla_scan(z, lb[1], lay, True)
            xs, h = _hg_readout(o_f, o_b, z, hg_gnorm_w[j], hg_w_out[j], xs, modt,
                                norm_ffn_g[layer], lay)
        else:
            xs, h = _rwkv7_mixer(h, xs, modt, norm_ffn_g[layer], lay, rw_mu[j], rw_w_rkv[j],
                                 rw_dec_w0[j], rw_dec_w1[j], rw_dec_w2[j], rw_iclr_a0[j],
                                 rw_iclr_a1[j], rw_iclr_a2[j], rw_g1[j], rw_g2[j], rw_k_k[j],
                                 rw_k_a[j], rw_r_k[j], rw_lnx_w[j], rw_lnx_b[j], rw_w_out[j])
        if last:
            n_rows = bsz * n_lat
            next_g, next_modt = final_g, jnp.zeros_like(modt)
        else:
            next_g, next_modt = norm_mix_g[layer + 1], modts[layer + 1]
        xs, h = _moe(h, n_rows, xs, modt, next_g, next_modt, lay, moe_router_w[layer],
                     moe_router_b[layer], moe_w_gu[layer], moe_b_gu[layer], moe_w_down[layer],
                     moe_b_down[layer])
    return h.reshape(bsz, n_lat, d)
```

```python
import functools

import jax
import jax.numpy as jnp
from jax import lax
from jax.experimental import pallas as pl
from jax.experimental.pallas import tpu as pltpu

F32 = jnp.float32
BF16 = jnp.bfloat16
HIGHEST = lax.Precision.HIGHEST

GRID_W = 64
HG_HEADS = 8
HG_DK = 128
RW_HEADS = 16
RW_DK = 64
N_EXPERTS = 32
TOP_K = 4
SWIGLU_LIMIT = 7.0
SWIGLU_ALPHA = 1.702
NORM_EPS = 1e-6
RW_LNX_EPS = 1e-5 * RW_DK

LANES = 128
CHUNK = 64
SUB = 16
ROW_BLOCK = 256
MOE_BLOCK = 256
VMEM_LIMIT = 56 * 1024 * 1024

_NT = (((1,), (1,)), ((), ()))


def _dot(a, b):
    return jnp.dot(a.astype(BF16), b.astype(BF16), preferred_element_type=F32)


def _dot_nt(a, b):
    return lax.dot_general(a.astype(BF16), b.astype(BF16), _NT, preferred_element_type=F32)


def _split(a):
    hi = a.astype(BF16)
    lo = (a - hi.astype(F32)).astype(BF16)
    return hi, lo


def _dot3(a, b):
    a_hi, a_lo = _split(a)
    b_hi, b_lo = _split(b)
    f = lambda x, y: jnp.dot(x, y, preferred_element_type=F32)
    return f(a_hi, b_hi) + (f(a_hi, b_lo) + f(a_lo, b_hi))


def _head_ones(width):
    r = lax.broadcasted_iota(jnp.int32, (LANES, LANES), 0) // width
    c = lax.broadcasted_iota(jnp.int32, (LANES, LANES), 1) // width
    return (r == c).astype(BF16)


def _head_sums(x, ones_blk):
    hi, lo = _split(x)
    outs = []
    for c in range(x.shape[1] // LANES):
        sl = slice(c * LANES, (c + 1) * LANES)
        outs.append(jnp.dot(hi[:, sl], ones_blk, preferred_element_type=F32)
                    + jnp.dot(lo[:, sl], ones_blk, preferred_element_type=F32))
    return jnp.concatenate(outs, axis=1)


def _norm_modulate(x, g, shift, scale):
    y = x * lax.rsqrt(jnp.mean(x * x, axis=-1, keepdims=True) + NORM_EPS) * g
    return y * (1.0 + scale) + shift


def _params(sem):
    return pltpu.CompilerParams(dimension_semantics=sem, vmem_limit_bytes=VMEM_LIMIT)


class _Layout:
    def __init__(self, bsz, n_lat, n_ctx):
        assert n_lat % ROW_BLOCK == 0 and n_ctx % ROW_BLOCK == 0
        self.bsz, self.n_lat, self.n_ctx = bsz, n_lat, n_ctx
        self.nlb = n_lat // ROW_BLOCK
        self.ncb = n_ctx // ROW_BLOCK
        self.rows = bsz * (n_lat + n_ctx)

    def segment(self, i):
        return jnp.minimum(i // self.nlb, self.bsz)

    def seq_block(self, b, c, rev):
        if rev:
            s = jnp.where(c < self.ncb, self.ncb - 1 - c, self.nlb + self.ncb - 1 - (c - self.ncb))
        else:
            s = c
        return jnp.where(s < self.ncb, self.bsz * self.nlb + b * self.ncb + s,
                         b * self.nlb + (s - self.ncb))


def _mm_body(x_ref, w_ref, b_ref, o_ref, wb_ref, *, act):
    @pl.when(pl.program_id(1) == 0)
    def _():
        wb_ref[...] = w_ref[...].astype(BF16)

    acc = jnp.dot(x_ref[...].astype(BF16), wb_ref[...], preferred_element_type=F32) + b_ref[...]
    if act == "tanh":
        acc = jnp.tanh(acc)
    elif act == "sigmoid":
        acc = jax.nn.sigmoid(acc)
    o_ref[...] = acc.astype(o_ref.dtype)


MM_ROW_TILES = (1536, 1024, 512, 256, 128, 64, 32, 16, 8)
MM_COL_TILES = (1024, 512, 256, 128)


def _mm(x, w, b=None, act=None, out_dtype=F32):
    m, k = x.shape
    n = w.shape[1]
    tm = next(t for t in MM_ROW_TILES if m % t == 0)
    tn = next((t for t in MM_COL_TILES if n % t == 0), n)
    if b is None:
        b = jnp.zeros((n,), F32)
    return pl.pallas_call(
        functools.partial(_mm_body, act=act),
        grid=(n // tn, m // tm),
        in_specs=[
            pl.BlockSpec((tm, k), lambda j, i: (i, 0)),
            pl.BlockSpec((k, tn), lambda j, i: (0, j)),
            pl.BlockSpec((1, tn), lambda j, i: (0, j)),
        ],
        out_specs=pl.BlockSpec((tm, tn), lambda j, i: (i, j)),
        out_shape=jax.ShapeDtypeStruct((m, n), out_dtype),
        scratch_shapes=[pltpu.VMEM((k, tn), BF16)],
        compiler_params=_params(("arbitrary", "arbitrary")),
        name="mm",
    )(x, w, b.reshape(1, n).astype(F32))


def _mm_exact_body(x_ref, w_ref, b_ref, o_ref):
    acc = jnp.dot(x_ref[...], w_ref[...], precision=HIGHEST, preferred_element_type=F32)
    o_ref[...] = acc + b_ref[...]


def _mm_exact(x, w, b, tn=512):
    m, k = x.shape
    n = w.shape[1]
    tn = min(tn, n)
    return pl.pallas_call(
        _mm_exact_body,
        grid=(n // tn,),
        in_specs=[
            pl.BlockSpec((m, k), lambda j: (0, 0)),
            pl.BlockSpec((k, tn), lambda j: (0, j)),
            pl.BlockSpec((1, tn), lambda j: (0, j)),
        ],
        out_specs=pl.BlockSpec((m, tn), lambda j: (0, j)),
        out_shape=jax.ShapeDtypeStruct((m, n), F32),
        compiler_params=_params(("arbitrary",)),
        name="mm_exact",
    )(x, w, b.reshape(1, n))


def _norm_mod_body(x_ref, g_ref, m_ref, o_ref):
    o_ref[...] = _norm_modulate(x_ref[...], g_ref[...], m_ref[0, 0:1], m_ref[0, 1:2]).astype(o_ref.dtype)


def _norm_mod(xs, g, modt, lay, out_dtype):
    m, d = xs.shape
    return pl.pallas_call(
        _norm_mod_body,
        grid=(m // ROW_BLOCK,),
        in_specs=[
            pl.BlockSpec((ROW_BLOCK, d), lambda i: (i, 0)),
            pl.BlockSpec((1, d), lambda i: (0, 0)),
            pl.BlockSpec((1,) + modt.shape[1:], lambda i: (lay.segment(i), 0, 0)),
        ],
        out_specs=pl.BlockSpec((ROW_BLOCK, d), lambda i: (i, 0)),
        out_shape=jax.ShapeDtypeStruct((m, d), out_dtype),
        compiler_params=_params(("arbitrary",)),
        name="norm_mod",
    )(xs, g.reshape(1, d), modt)


HG_HB = 4


def _gla_chunk(q, f, v, lb, sts, rev, consts):
    tri, ones_kk, trow = consts
    heads = range(len(sts))
    hs = [slice(h * HG_DK, (h + 1) * HG_DK) for h in heads]
    q = q * jax.nn.sigmoid(q)
    g = jnp.log(lb + (1.0 - lb) * jax.nn.sigmoid(f))
    k = (1.0 - lb) * jax.nn.sigmoid(-f)
    b = jnp.dot(tri, g, precision=HIGHEST, preferred_element_type=F32)
    b_end = b[0:1] if rev else b[CHUNK - 1:CHUNK]
    q_in = q * jnp.exp(b)
    k_end = k * jnp.exp(b_end - b)
    o_inter = [_dot_nt(q_in[:, s], sts[h]) for h, s in zip(heads, hs)]
    kv = [_dot(v[:, s].T, k_end[:, s]) for s in hs]
    e_end = jnp.exp(b_end)
    sts_new = [sts[h] * e_end[:, hs[h]] + kv[h] for h in heads]

    outs = [None] * (CHUNK // SUB)
    for p in range(CHUNK // SUB):
        if rev:
            r0 = CHUNK - SUB * (p + 1)
            e0, e1, mrow = CHUNK - SUB * p, CHUNK, CHUNK - SUB * p
        else:
            r0 = SUB * p
            e0, e1, mrow = 0, SUB * p, SUB * p - 1
        qp, bp, kp, vp = (x[r0:r0 + SUB] for x in (q, b, k, v))
        acc = [o_inter[h][r0:r0 + SUB] for h in heads]
        if p > 0:
            m = b[mrow:mrow + 1]
            q_lo = qp * jnp.exp(bp - m)
            k_hi = k[e0:e1] * jnp.exp(m - b[e0:e1])
            sc = [_dot_nt(q_lo[:, s], k_hi[:, s]) for s in hs]
            acc = [acc[h] + _dot(sc[h], v[e0:e1, hs[h]]) for h in heads]
        terms = []
        for s in range(SUB):
            mask = (trow <= s) if rev else (trow >= s)
            w = jnp.exp(jnp.where(mask, bp - bp[s:s + 1], -1e30))
            terms.append((qp * kp[s:s + 1] * w).astype(BF16))
        terms = jnp.concatenate(terms, axis=0)
        sums = [jnp.dot(terms[:, s], ones_kk, preferred_element_type=F32) for s in hs]
        for s in range(SUB):
            acc = [acc[h] + sums[h][s * SUB:(s + 1) * SUB] * vp[s:s + 1, hs[h]] for h in heads]
        outs[r0 // SUB] = jnp.concatenate(acc, axis=1)
    return jnp.concatenate(outs, axis=0), sts_new


def _gla_body(q_ref, f_ref, v_ref, lb_ref, o_ref, st_ref, *, rev):
    @pl.when(pl.program_id(2) == 0)
    def _():
        st_ref[...] = jnp.zeros_like(st_ref)

    row = lax.broadcasted_iota(jnp.int32, (CHUNK, CHUNK), 0)
    col = lax.broadcasted_iota(jnp.int32, (CHUNK, CHUNK), 1)
    tri = ((row <= col) if rev else (row >= col)).astype(F32)
    ones_kk = jnp.ones((HG_DK, HG_DK), BF16)
    trow = lax.broadcasted_iota(jnp.int32, (SUB, HG_HB * HG_DK), 0)
    lb = lb_ref[0]
    n_chunks = ROW_BLOCK // CHUNK

    def step(i, carry):
        cc = (n_chunks - 1 - i) if rev else i
        sl = pl.ds(pl.multiple_of(cc * CHUNK, CHUNK), CHUNK)
        o, sts = _gla_chunk(q_ref[sl, :].astype(F32), f_ref[sl, :].astype(F32),
                            v_ref[sl, :].astype(F32), lb,
                            [st_ref[h] for h in range(HG_HB)], rev, (tri, ones_kk, trow))
        o_ref[sl, :] = o
        for h in range(HG_HB):
            st_ref[h] = sts[h]
        return carry

    lax.fori_loop(0, n_chunks, step, 0)


def _gla_scan(z, lb, lay, rev):
    m = z.shape[0]
    d_model = HG_HEADS * HG_DK
    ng = HG_HEADS // HG_HB
    width = HG_HB * HG_DK
    f_off = ng * (2 if rev else 1)
    v_off = ng * 3
    rmap = lambda b, c: lay.seq_block(b, c, rev)
    return pl.pallas_call(
        functools.partial(_gla_body, rev=rev),
        grid=(lay.bsz, ng, lay.nlb + lay.ncb),
        in_specs=[
            pl.BlockSpec((ROW_BLOCK, width), lambda b, h, c: (rmap(b, c), h)),
            pl.BlockSpec((ROW_BLOCK, width), lambda b, h, c: (rmap(b, c), f_off + h)),
            pl.BlockSpec((ROW_BLOCK, width), lambda b, h, c: (rmap(b, c), v_off + h)),
            pl.BlockSpec((1, 1, width), lambda b, h, c: (h, 0, 0)),
        ],
        out_specs=pl.BlockSpec((ROW_BLOCK, width), lambda b, h, c: (rmap(b, c), h)),
        out_shape=jax.ShapeDtypeStruct((m, d_model), F32),
        scratch_shapes=[pltpu.VMEM((HG_HB, HG_DK, HG_DK), F32)],
        compiler_params=_params(("arbitrary", "arbitrary", "arbitrary")),
        name="gla_rev" if rev else "gla_fwd",
    )(z, z, z, lb.reshape(ng, 1, width))


def _hg_readout_body(of_ref, ob_ref, g_ref, gw_ref, w_ref, xs_ref, m_ref, ng_ref,
                     xo_ref, ho_ref, wb_ref):
    @pl.when(pl.program_id(0) == 0)
    def _():
        wb_ref[...] = w_ref[...].astype(BF16)

    o = of_ref[...] + ob_ref[...]
    parts = []
    for h in range(HG_HEADS):
        blk = o[:, h * HG_DK:(h + 1) * HG_DK]
        ms = jnp.mean(blk * blk, axis=-1, keepdims=True)
        parts.append(blk * lax.rsqrt(ms + NORM_EPS))
    gate = g_ref[...].astype(F32)
    y = jnp.concatenate(parts, axis=1) * gw_ref[...] * (gate * jax.nn.sigmoid(gate))
    acc = jnp.dot(y.astype(BF16), wb_ref[...], preferred_element_type=F32)
    xs_new = xs_ref[...] + m_ref[0, 2:3] * acc
    xo_ref[...] = xs_new
    ho_ref[...] = _norm_modulate(xs_new, ng_ref[...], m_ref[0, 3:4], m_ref[0, 4:5])


def _hg_readout(o_f, o_b, z, gnorm_w, w_out, xs, modt, norm_g, lay):
    m, d = xs.shape
    row = pl.BlockSpec((ROW_BLOCK, d), lambda i: (i, 0))
    vec = pl.BlockSpec((1, d), lambda i: (0, 0))
    return pl.pallas_call(
        _hg_readout_body,
        grid=(m // ROW_BLOCK,),
        in_specs=[row, row,
                  pl.BlockSpec((ROW_BLOCK, d), lambda i: (i, 4)),
                  vec,
                  pl.BlockSpec((d, d), lambda i: (0, 0)),
                  row,
                  pl.BlockSpec((1,) + modt.shape[1:], lambda i: (lay.segment(i), 0, 0)),
                  vec],
        out_specs=[row, row],
        out_shape=[jax.ShapeDtypeStruct((m, d), F32), jax.ShapeDtypeStruct((m, d), F32)],
        scratch_shapes=[pltpu.VMEM((d, d), BF16)],
        compiler_params=_params(("arbitrary",)),
        name="hg_readout",
    )(o_f, o_b, z, jnp.tile(gnorm_w, HG_HEADS).reshape(1, d), w_out, xs, modt,
      norm_g.reshape(1, d))


RW_HB = 16


def _rwkv_chunk(lw, cum, r, k, v, kap, a, sts, rev, consts):
    incl, strict, eye = consts
    nh = len(sts)
    heads = range(nh)
    hs = [slice(h * RW_DK, (h + 1) * RW_DK) for h in heads]
    cum_ex = cum - lw
    tot = cum[0:1] if rev else cum[CHUNK - 1:CHUNK]
    e_neg = jnp.exp(-cum)
    kt = kap * jnp.exp(cum_ex)
    bb = kap * a
    bt = bb * e_neg
    kd = k * e_neg
    rt = r * jnp.exp(cum)
    dec_end = jnp.exp(tot - cum)
    b_end = bb * dec_end
    k_end = k * dec_end
    e_tot = jnp.exp(tot)

    gram = [_dot_nt(jnp.concatenate([kt[:, s], rt[:, s]], axis=0),
                    jnp.concatenate([bt[:, s], kd[:, s]], axis=0)) for s in hs]
    a_bk = [jnp.where(strict, g[:CHUNK, :CHUNK], 0.0) for g in gram]
    a_kr = [jnp.concatenate([jnp.where(strict, g[:CHUNK, CHUNK:], 0.0),
                             jnp.where(incl, g[CHUNK:, CHUNK:], 0.0)], axis=0) for g in gram]
    a_rb = [jnp.where(incl, g[CHUNK:, :CHUNK], 0.0) for g in gram]
    av = [_dot(a_kr[h], v[:, hs[h]]) for h in heads]

    tinv = [eye - x for x in a_bk]
    pw = [_dot(x, x) for x in a_bk]
    for _ in range(4):
        tp = [_dot(jnp.concatenate([tinv[h], pw[h]], axis=0), pw[h]) for h in heads]
        tinv = [tinv[h] + tp[h][:CHUNK] for h in heads]
        pw = [x[CHUNK:] for x in tp]
    tinv = [tinv[h] + _dot(tinv[h], pw[h]) for h in heads]

    ku = [_dot(tinv[h], jnp.concatenate([kt[:, hs[h]], av[h][:CHUNK]], axis=1)) for h in heads]
    ab = [_dot(a_rb[h], ku[h]) for h in heads]
    rh = [rt[:, hs[h]] - ab[h][:, :RW_DK] for h in heads]
    y0 = [av[h][CHUNK:] - ab[h][:, RW_DK:] for h in heads]
    ys = [_dot_nt(rh[h], sts[h]) + y0[h] for h in heads]
    mb = [_dot(ku[h].T, b_end[:, hs[h]]) for h in heads]
    m_mat = [jnp.where(eye > 0, e_tot[:, hs[h]], 0.0) - mb[h][:RW_DK] for h in heads]
    n_mat = [_dot(v[:, hs[h]].T, k_end[:, hs[h]]) - mb[h][RW_DK:] for h in heads]
    sts_new = [_dot3(sts[h], m_mat[h]) + n_mat[h] for h in heads]
    return jnp.concatenate(ys, axis=1), sts_new


def _rwkv_body(r_ref, k_ref, v_ref, lw_ref, la_ref, w2_ref, w0_ref, a2_ref, a0_ref,
               kk_ref, ka_ref, rk_ref, y_ref, bv_ref, st_ref, *, rev):
    @pl.when(pl.program_id(2) == 0)
    def _():
        st_ref[...] = jnp.zeros_like(st_ref)

    row = lax.broadcasted_iota(jnp.int32, (CHUNK, CHUNK), 0)
    col = lax.broadcasted_iota(jnp.int32, (CHUNK, CHUNK), 1)
    incl = (row <= col) if rev else (row >= col)
    strict = (row < col) if rev else (row > col)
    eye = (row == col).astype(F32)
    tri = incl.astype(F32)
    ones_blk = _head_ones(RW_DK)
    n_chunks = ROW_BLOCK // CHUNK

    def step(i, carry):
        cc = (n_chunks - 1 - i) if rev else i
        sl = pl.ds(pl.multiple_of(cc * CHUNK, CHUNK), CHUNK)
        r, k, v = (ref[sl, :].astype(F32) for ref in (r_ref, k_ref, v_ref))
        lo_w = (1 if rev else 0) * (lw_ref.shape[1] // 2)
        lo_a = (1 if rev else 0) * (la_ref.shape[1] // 2)
        zw = jnp.dot(lw_ref[sl, :][:, lo_w:lo_w + w2_ref.shape[0]], w2_ref[...].astype(BF16),
                     preferred_element_type=F32) + w0_ref[...]
        za = jnp.dot(la_ref[sl, :][:, lo_a:lo_a + a2_ref.shape[0]], a2_ref[...].astype(BF16),
                     preferred_element_type=F32) + a0_ref[...]
        kmul = k * kk_ref[...]
        kap = kmul * lax.rsqrt(jnp.maximum(_head_sums(kmul * kmul, ones_blk), 1e-24))
        a = jax.nn.sigmoid(za)
        lw = -jnp.exp(-jax.nn.softplus(-zw) - 0.5)
        k_dir = k * (1.0 + (a - 1.0) * ka_ref[...])
        cum = jnp.dot(tri, lw, precision=HIGHEST, preferred_element_type=F32)
        y, sts = _rwkv_chunk(lw, cum, r, k_dir, v, kap, a, [st_ref[h] for h in range(RW_HB)],
                             rev, (incl, strict, eye))
        for h in range(RW_HB):
            st_ref[h] = sts[h]
        y_ref[sl, :] = y
        bv_ref[sl, :] = _head_sums(r * k_dir * rk_ref[...], ones_blk) * v
        return carry

    lax.fori_loop(0, n_chunks, step, 0)


def _rwkv_scan(r, k, v, lora_w, lora_a, w2, w0, a2, a0, k_k, k_a, r_k, lay, rev):
    m, d_model = r.shape
    width = RW_HB * RW_DK
    rmap = lambda b, c: lay.seq_block(b, c, rev)
    spec = pl.BlockSpec((ROW_BLOCK, width), lambda b, h, c: (rmap(b, c), h))
    lora = lambda x: pl.BlockSpec((ROW_BLOCK, x.shape[1]), lambda b, h, c: (rmap(b, c), 0))
    mat = lambda x: pl.BlockSpec((x.shape[0], width), lambda b, h, c: (0, h))
    vec = pl.BlockSpec((1, width), lambda b, h, c: (0, h))
    out = jax.ShapeDtypeStruct((m, d_model), F32)
    row = lambda x: x.reshape(1, d_model)
    return pl.pallas_call(
        functools.partial(_rwkv_body, rev=rev),
        grid=(lay.bsz, d_model // width, lay.nlb + lay.ncb),
        in_specs=[spec] * 3 + [lora(lora_w), lora(lora_a), mat(w2), vec, mat(a2), vec] + [vec] * 3,
        out_specs=[spec, spec],
        out_shape=[out, out],
        scratch_shapes=[pltpu.VMEM((RW_HB, RW_DK, RW_DK), F32)],
        compiler_params=_params(("arbitrary", "arbitrary", "arbitrary")),
        name="rwkv_rev" if rev else "rwkv_fwd",
    )(r, k, v, lora_w, lora_a, w2, row(w0), a2, row(a0), row(k_k), row(k_a), row(r_k))


def _rw_readout_body(yf_ref, yb_ref, bf_ref, bb_ref, g_ref, lw_ref, lb_ref, w_ref, xs_ref,
                     m_ref, ng_ref, xo_ref, ho_ref, wb_ref):
    @pl.when(pl.program_id(0) == 0)
    def _():
        wb_ref[...] = w_ref[...].astype(BF16)

    ones_blk = _head_ones(RW_DK)
    y = yf_ref[...] + yb_ref[...]
    dev = y - _head_sums(y, ones_blk) * (1.0 / RW_DK)
    var = _head_sums(dev * dev, ones_blk) * (1.0 / RW_DK)
    yn = dev * lax.rsqrt(var + RW_LNX_EPS) * lw_ref[...] + lb_ref[...]
    yn = yn + bf_ref[...] + bb_ref[...]
    acc = jnp.dot((yn * g_ref[...].astype(F32)).astype(BF16), wb_ref[...],
                  preferred_element_type=F32)
    xs_new = xs_ref[...] + m_ref[0, 2:3] * acc
    xo_ref[...] = xs_new
    ho_ref[...] = _norm_modulate(xs_new, ng_ref[...], m_ref[0, 3:4], m_ref[0, 4:5])


def _rw_readout(y_f, y_b, bv_f, bv_b, gate, lnx_w, lnx_b, w_out, xs, modt, norm_g, lay):
    m, d = xs.shape
    row = pl.BlockSpec((ROW_BLOCK, d), lambda i: (i, 0))
    vec = pl.BlockSpec((1, d), lambda i: (0, 0))
    return pl.pallas_call(
        _rw_readout_body,
        grid=(m // ROW_BLOCK,),
        in_specs=[row] * 5 + [vec, vec, pl.BlockSpec((d, d), lambda i: (0, 0)), row,
                              pl.BlockSpec((1,) + modt.shape[1:],
                                           lambda i: (lay.segment(i), 0, 0)),
                              vec],
        out_specs=[row, row],
        out_shape=[jax.ShapeDtypeStruct((m, d), F32), jax.ShapeDtypeStruct((m, d), F32)],
        scratch_shapes=[pltpu.VMEM((d, d), BF16)],
        compiler_params=_params(("arbitrary",)),
        name="rw_readout",
    )(y_f, y_b, bv_f, bv_b, gate, lnx_w.reshape(1, d), lnx_b.reshape(1, d), w_out, xs, modt,
      norm_g.reshape(1, d))


def _router_body(x_ref, w_ref, b_ref, idx_ref, gate_ref):
    logits = _dot3(x_ref[...], w_ref[...]) + b_ref[...]
    lane = lax.broadcasted_iota(jnp.int32, logits.shape, 1)
    vals, idxs = [], []
    for _ in range(TOP_K):
        m = jnp.max(logits, axis=-1, keepdims=True)
        idx = jnp.min(jnp.where(logits == m, lane, N_EXPERTS), axis=-1, keepdims=True)
        vals.append(m)
        idxs.append(idx)
        logits = jnp.where(lane == idx, -jnp.inf, logits)
    exps = [jnp.exp(val - vals[0]) for val in vals]
    denom = exps[0] + exps[1] + exps[2] + exps[3]
    idx_ref[...] = jnp.concatenate(idxs, axis=1)
    gate_ref[...] = jnp.concatenate(exps, axis=1) / denom


def _router(x, n, w, b):
    d = x.shape[1]
    return pl.pallas_call(
        _router_body,
        grid=(n // ROW_BLOCK,),
        in_specs=[
            pl.BlockSpec((ROW_BLOCK, d), lambda i: (i, 0)),
            pl.BlockSpec((d, N_EXPERTS), lambda i: (0, 0)),
            pl.BlockSpec((1, N_EXPERTS), lambda i: (0, 0)),
        ],
        out_specs=[pl.BlockSpec((ROW_BLOCK, TOP_K), lambda i: (i, 0)),
                   pl.BlockSpec((ROW_BLOCK, TOP_K), lambda i: (i, 0))],
        out_shape=[jax.ShapeDtypeStruct((n, TOP_K), jnp.int32),
                   jax.ShapeDtypeStruct((n, TOP_K), F32)],
        compiler_params=_params(("arbitrary",)),
        name="router",
    )(x, w, b.reshape(1, N_EXPERTS))


EXPERT_COLS = 512


def _expert_body(be_ref, nu_ref, x_ref, wgu_ref, bgu_ref, wd_ref, bd_ref, o_ref,
                 wgu_b, wd_b):
    i = pl.program_id(0)
    d_exp = wd_ref.shape[1]
    prev = be_ref[jnp.maximum(i - 1, 0)]

    @pl.when((i == 0) | (be_ref[i] != prev))
    def _():
        wgu_b[...] = wgu_ref[0].astype(BF16)
        wd_b[...] = wd_ref[0].astype(BF16)

    @pl.when(i < nu_ref[0])
    def _():
        x = x_ref[...].astype(BF16)
        acc = jnp.zeros(o_ref.shape, F32)
        for c0 in range(0, d_exp, EXPERT_COLS):
            gate = jnp.dot(x, wgu_b[:, c0:c0 + EXPERT_COLS], preferred_element_type=F32)
            gate = gate + bgu_ref[0, :, c0:c0 + EXPERT_COLS]
            up = jnp.dot(x, wgu_b[:, d_exp + c0:d_exp + c0 + EXPERT_COLS],
                         preferred_element_type=F32)
            up = up + bgu_ref[0, :, d_exp + c0:d_exp + c0 + EXPERT_COLS]
            gate = jnp.minimum(gate, SWIGLU_LIMIT)
            up = jnp.clip(up, -SWIGLU_LIMIT, SWIGLU_LIMIT)
            act = gate * jax.nn.sigmoid(SWIGLU_ALPHA * gate) * (up + 1.0)
            acc = acc + jnp.dot(act.astype(BF16), wd_b[c0:c0 + EXPERT_COLS, :],
                                preferred_element_type=F32)
        o_ref[...] = acc + bd_ref[0]

    @pl.when(i >= nu_ref[0])
    def _():
        o_ref[...] = jnp.zeros_like(o_ref)


def _experts(xs, block_e, n_used, w_gu, b_gu, w_down, b_down):
    cap, d = xs.shape
    n_e, _, d2 = w_gu.shape
    d_exp = d2 // 2
    nb = cap // MOE_BLOCK
    grid_spec = pltpu.PrefetchScalarGridSpec(
        num_scalar_prefetch=2,
        grid=(nb,),
        in_specs=[
            pl.BlockSpec((MOE_BLOCK, d), lambda i, be, nu: (i, 0)),
            pl.BlockSpec((1, d, d2), lambda i, be, nu: (be[i], 0, 0)),
            pl.BlockSpec((1, 1, d2), lambda i, be, nu: (be[i], 0, 0)),
            pl.BlockSpec((1, d_exp, d), lambda i, be, nu: (be[i], 0, 0)),
            pl.BlockSpec((1, 1, d), lambda i, be, nu: (be[i], 0, 0)),
        ],
        out_specs=pl.BlockSpec((MOE_BLOCK, d), lambda i, be, nu: (i, 0)),
        scratch_shapes=[pltpu.VMEM((d, d2), BF16), pltpu.VMEM((d_exp, d), BF16)],
    )
    return pl.pallas_call(
        _expert_body,
        grid_spec=grid_spec,
        out_shape=jax.ShapeDtypeStruct((cap, d), F32),
        compiler_params=_params(("arbitrary",)),
        name="experts",
    )(block_e, n_used, xs, w_gu, b_gu.reshape(n_e, 1, d2), w_down, b_down.reshape(n_e, 1, d))


def _row_copies(make_copy, start):
    def body(r, carry):
        for j in range(TOP_K):
            cp = make_copy(r, j)
            cp.start(priority=j % 2) if start else cp.wait()
        return carry

    lax.fori_loop(0, ROW_BLOCK, body, 0, unroll=8)


def _dispatch_body(dst_ref, x_ref, init_ref, xs_ref, sem):
    del init_ref

    def copy(r, j):
        return pltpu.make_async_copy(x_ref.at[pl.ds(r, 1)],
                                     xs_ref.at[pl.ds(dst_ref[0, 0, r * TOP_K + j], 1)], sem)

    _row_copies(copy, True)
    _row_copies(copy, False)


def _dispatch(tokens, n, dest, cap):
    d = tokens.shape[1]
    nblk = n // ROW_BLOCK
    return pl.pallas_call(
        _dispatch_body,
        grid=(nblk,),
        in_specs=[
            pl.BlockSpec((1, 1, ROW_BLOCK * TOP_K), lambda i: (i, 0, 0),
                         memory_space=pltpu.SMEM),
            pl.BlockSpec((ROW_BLOCK, d), lambda i: (i, 0)),
            pl.BlockSpec(memory_space=pl.ANY),
        ],
        out_specs=pl.BlockSpec(memory_space=pl.ANY),
        out_shape=jax.ShapeDtypeStruct((cap, d), F32),
        scratch_shapes=[pltpu.SemaphoreType.DMA(())],
        input_output_aliases={2: 0},
        compiler_params=_params(("arbitrary",)),
        name="dispatch",
    )(dest.reshape(nblk, 1, ROW_BLOCK * TOP_K), tokens, jnp.zeros((cap, d), F32))


def _combine_body(dst_ref, gate_ref, xs_ref, m_ref, ng_ref, m2_ref, ys_ref, xo_ref, ho_ref,
                  buf, sem):
    def copy(r, j):
        return pltpu.make_async_copy(ys_ref.at[pl.ds(dst_ref[0, 0, r * TOP_K + j], 1)],
                                     buf.at[j, pl.ds(r, 1)], sem)

    _row_copies(copy, True)
    _row_copies(copy, False)
    gates = gate_ref[...]
    acc = gates[:, 0:1] * buf[0]
    for j in range(1, TOP_K):
        acc = acc + gates[:, j:j + 1] * buf[j]
    xs_new = xs_ref[...] + m_ref[0, 5:6] * acc
    xo_ref[...] = xs_new
    ho_ref[...] = _norm_modulate(xs_new, ng_ref[...], m2_ref[0, 0:1], m2_ref[0, 1:2])


def _combine(ys, dest, gates, xs, modt, next_g, next_modt, lay):
    n = gates.shape[0]
    d = ys.shape[1]
    nblk = n // ROW_BLOCK
    row = pl.BlockSpec((ROW_BLOCK, d), lambda i: (i, 0))
    seg = lambda mt: pl.BlockSpec((1,) + mt.shape[1:], lambda i: (lay.segment(i), 0, 0))
    return pl.pallas_call(
        _combine_body,
        grid=(nblk,),
        in_specs=[
            pl.BlockSpec((1, 1, ROW_BLOCK * TOP_K), lambda i: (i, 0, 0),
                         memory_space=pltpu.SMEM),
            pl.BlockSpec((ROW_BLOCK, TOP_K), lambda i: (i, 0)),
            row, seg(modt), pl.BlockSpec((1, d), lambda i: (0, 0)), seg(next_modt),
            pl.BlockSpec(memory_space=pl.ANY),
        ],
        out_specs=[row, row],
        out_shape=[jax.ShapeDtypeStruct((n, d), F32), jax.ShapeDtypeStruct((n, d), F32)],
        scratch_shapes=[pltpu.VMEM((TOP_K, ROW_BLOCK, d), F32), pltpu.SemaphoreType.DMA(())],
        compiler_params=_params(("arbitrary",)),
        name="combine",
    )(dest.reshape(nblk, 1, ROW_BLOCK * TOP_K), gates, xs, modt, next_g.reshape(1, d),
      next_modt, ys)


def _moe(h, n, xs, modt, next_g, next_modt, lay, router_w, router_b, w_gu, b_gu, w_down, b_down):
    d = h.shape[1]
    top_idx, gates = _router(h, n, router_w, router_b)
    flat_e = top_idx.reshape(-1)
    onehot = (flat_e[:, None] == jnp.arange(N_EXPERTS, dtype=jnp.int32)[None, :]).astype(jnp.int32)
    rank = jnp.sum((jnp.cumsum(onehot, axis=0) - 1) * onehot, axis=1)
    counts = jnp.sum(onehot, axis=0)
    padded = (counts + MOE_BLOCK - 1) // MOE_BLOCK * MOE_BLOCK
    padded_end = jnp.cumsum(padded)
    padded_start = padded_end - padded
    dest = jnp.sum(onehot * padded_start[None, :], axis=1) + rank
    n_blocks = -(-(n * TOP_K + N_EXPERTS * (MOE_BLOCK - 1)) // MOE_BLOCK)
    cap = n_blocks * MOE_BLOCK
    n_used = (padded_end[-1] // MOE_BLOCK).astype(jnp.int32)
    blk = jnp.minimum(jnp.arange(n_blocks, dtype=jnp.int32), n_used - 1) * MOE_BLOCK
    block_e = jnp.minimum(jnp.sum((padded_end[None, :] <= blk[:, None]).astype(jnp.int32), axis=1),
                          N_EXPERTS - 1)
    slots = _dispatch(h, n, dest, cap)
    ys = _experts(slots, block_e, n_used.reshape(1), w_gu, b_gu, w_down, b_down)
    return _combine(ys, dest, gates, xs, modt, next_g, next_modt, lay)


def _shift_grid(h):
    bsz, t, d = h.shape
    g = h.reshape(bsz, t // GRID_W, GRID_W, d)
    c = d // 4
    left = jnp.pad(g[:, :, :-1, :c], ((0, 0), (0, 0), (1, 0), (0, 0)))
    right = jnp.pad(g[:, :, 1:, c:2 * c], ((0, 0), (0, 0), (0, 1), (0, 0)))
    up = jnp.pad(g[:, :-1, :, 2 * c:3 * c], ((0, 0), (1, 0), (0, 0), (0, 0)))
    down = jnp.pad(g[:, 1:, :, 3 * c:], ((0, 0), (0, 1), (0, 0), (0, 0)))
    return jnp.concatenate([left, right, up, down], axis=-1).reshape(bsz, t, d)


def _shift_seq(h):
    c = h.shape[-1] // 2
    prev = jnp.pad(h[:, :-1, :c], ((0, 0), (1, 0), (0, 0)))
    nxt = jnp.pad(h[:, 1:, c:], ((0, 0), (0, 1), (0, 0)))
    return jnp.concatenate([prev, nxt], axis=-1)


def _rwkv7_mixer(h, xs, modt, norm_g, lay, mu, w_rkv, dec_w0, dec_w1, dec_w2, iclr_a0, iclr_a1,
                 iclr_a2, g1, g2, k_k, k_a, r_k, lnx_w, lnx_b, w_out):
    m, d = h.shape
    n_l = lay.bsz * lay.n_lat
    h_shift = jnp.concatenate(
        [_shift_grid(h[:n_l].reshape(lay.bsz, lay.n_lat, d)).reshape(n_l, d),
         _shift_seq(h[n_l:].reshape(lay.bsz, lay.n_ctx, d)).reshape(m - n_l, d)], axis=0)
    dx = h_shift - h
    xr, xw, xk, xv, xa, xg = ((h + dx * mu[j]).astype(BF16) for j in range(6))
    r = _mm(xr, w_rkv[0], out_dtype=BF16)
    k = _mm(xk, w_rkv[1], out_dtype=BF16)
    v = _mm(xv, w_rkv[2], out_dtype=BF16)
    lora_w = _mm(xw, jnp.concatenate([dec_w1[0], dec_w1[1]], axis=1), act="tanh", out_dtype=BF16)
    lora_a = _mm(xa, jnp.concatenate([iclr_a1[0], iclr_a1[1]], axis=1), out_dtype=BF16)
    gate = _mm(_mm(xg, g1, act="sigmoid", out_dtype=BF16), g2, out_dtype=BF16)
    ys = []
    for dd, rev in enumerate((False, True)):
        ys.append(_rwkv_scan(r, k, v, lora_w, lora_a, dec_w2[dd], dec_w0[dd], iclr_a2[dd],
                             iclr_a0[dd], k_k, k_a, r_k, lay, rev))
    return _rw_readout(ys[0][0], ys[1][0], ys[0][1], ys[1][1], gate, lnx_w, lnx_b, w_out,
                       xs, modt, norm_g, lay)


def kernel(x, c, ctx, c_ctx, ada_w, ada_b, norm_mix_g, norm_ffn_g, hg_w_in, hg_gnorm_w, hg_w_out, hg_lb, rw_mu, rw_w_rkv, rw_dec_w0, rw_dec_w1, rw_dec_w2, rw_iclr_a0, rw_iclr_a1, rw_iclr_a2, rw_g1, rw_g2, rw_k_k, rw_k_a, rw_r_k, rw_lnx_w, rw_lnx_b, rw_w_out, moe_router_w, moe_router_b, moe_w_gu, moe_b_gu, moe_w_down, moe_b_down, final_g):
    bsz, n_lat, d = x.shape
    n_ctx = ctx.shape[1]
    depth = ada_w.shape[0]
    lay = _Layout(bsz, n_lat, n_ctx)
    lb_all = jnp.cumsum(jax.nn.softmax(hg_lb.astype(F32), axis=1), axis=1)
    cond = jnp.concatenate([jax.nn.silu(c), jax.nn.silu(c_ctx)[None, :],
                            jnp.zeros((8 - bsz - 1, d), F32)], axis=0)
    modts = [_mm_exact(cond, ada_w[l], ada_b[l])[:bsz + 1].reshape(bsz + 1, 6, d)
             for l in range(depth)]
    xs = jnp.concatenate([x.reshape(bsz * n_lat, d), ctx.reshape(bsz * n_ctx, d)], axis=0)
    h = _norm_mod(xs, norm_mix_g[0], modts[0], lay, BF16)
    n_rows = lay.rows
    for layer in range(depth):
        last = layer == depth - 1
        modt = modts[layer]
        j = layer // 2
        if layer % 2 == 0:
            z = _mm(h.astype(BF16), hg_w_in[j], out_dtype=BF16)
            lb = lb_all[:, layer]
            o_f = _gla_scan(z, lb[0], lay, False)
            o_b = _gla_scan(z, lb[1], lay, True)
            xs, h = _hg_readout(o_f, o_b, z, hg_gnorm_w[j], hg_w_out[j], xs, modt,
                                norm_ffn_g[layer], lay)
        else:
            xs, h = _rwkv7_mixer(h, xs, modt, norm_ffn_g[layer], lay, rw_mu[j], rw_w_rkv[j],
                                 rw_dec_w0[j], rw_dec_w1[j], rw_dec_w2[j], rw_iclr_a0[j],
                                 rw_iclr_a1[j], rw_iclr_a2[j], rw_g1[j], rw_g2[j], rw_k_k[j],
                                 rw_k_a[j], rw_r_k[j], rw_lnx_w[j], rw_lnx_b[j], rw_w_out[j])
        if last:
            n_rows = bsz * n_lat
            next_g, next_modt = final_g, jnp.zeros_like(modt)
        else:
            next_g, next_modt = norm_mix_g[layer + 1], modts[layer + 1]
        xs, h = _moe(h, n_rows, xs, modt, next_g, next_modt, lay, moe_router_w[layer],
                     moe_router_b[layer], moe_w_gu[layer], moe_b_gu[layer], moe_w_down[layer],
                     moe_b_down[layer])
    return h.reshape(bsz, n_lat, d)
```

```python
import functools

import jax
import jax.numpy as jnp
from jax import lax
from jax.experimental import pallas as pl
from jax.experimental.pallas import tpu as pltpu

F32 = jnp.float32
BF16 = jnp.bfloat16
HIGHEST = lax.Precision.HIGHEST

GRID_W = 64
HG_HEADS = 8
HG_DK = 128
RW_HEADS = 16
RW_DK = 64
N_EXPERTS = 32
TOP_K = 4
SWIGLU_LIMIT = 7.0
SWIGLU_ALPHA = 1.702
NORM_EPS = 1e-6
RW_LNX_EPS = 1e-5 * RW_DK

LANES = 128
CHUNK = 64
SUB = 16
ROW_BLOCK = 256
MOE_BLOCK = 256
SUBLANES = 8
PAD_FILL = MOE_BLOCK + SUBLANES
VMEM_LIMIT = 56 * 1024 * 1024

_NT = (((1,), (1,)), ((), ()))


def _dot(a, b):
    return jnp.dot(a.astype(BF16), b.astype(BF16), preferred_element_type=F32)


def _dot_nt(a, b):
    return lax.dot_general(a.astype(BF16), b.astype(BF16), _NT, preferred_element_type=F32)


def _split(a):
    hi = a.astype(BF16)
    lo = (a - hi.astype(F32)).astype(BF16)
    return hi, lo


def _dot3(a, b):
    a_hi, a_lo = _split(a)
    b_hi, b_lo = _split(b)
    f = lambda x, y: jnp.dot(x, y, preferred_element_type=F32)
    return f(a_hi, b_hi) + (f(a_hi, b_lo) + f(a_lo, b_hi))


def _head_ones(width):
    r = lax.broadcasted_iota(jnp.int32, (LANES, LANES), 0) // width
    c = lax.broadcasted_iota(jnp.int32, (LANES, LANES), 1) // width
    return (r == c).astype(BF16)


def _head_sums(x, ones_blk):
    hi, lo = _split(x)
    outs = []
    for c in range(x.shape[1] // LANES):
        sl = slice(c * LANES, (c + 1) * LANES)
        outs.append(jnp.dot(hi[:, sl], ones_blk, preferred_element_type=F32)
                    + jnp.dot(lo[:, sl], ones_blk, preferred_element_type=F32))
    return jnp.concatenate(outs, axis=1)


def _norm_modulate(x, g, shift, scale):
    y = x * lax.rsqrt(jnp.mean(x * x, axis=-1, keepdims=True) + NORM_EPS) * g
    return y * (1.0 + scale) + shift


def _params(sem):
    return pltpu.CompilerParams(dimension_semantics=sem, vmem_limit_bytes=VMEM_LIMIT)


class _Layout:
    def __init__(self, bsz, n_lat, n_ctx):
        assert n_lat % ROW_BLOCK == 0 and n_ctx % ROW_BLOCK == 0
        self.bsz, self.n_lat, self.n_ctx = bsz, n_lat, n_ctx
        self.nlb = n_lat // ROW_BLOCK
        self.ncb = n_ctx // ROW_BLOCK
        self.rows = bsz * (n_lat + n_ctx)

    def segment(self, i):
        return jnp.minimum(i // self.nlb, self.bsz)

    def seq_block(self, b, c, rev):
        if rev:
            s = jnp.where(c < self.ncb, self.ncb - 1 - c, self.nlb + self.ncb - 1 - (c - self.ncb))
        else:
            s = c
        return jnp.where(s < self.ncb, self.bsz * self.nlb + b * self.ncb + s,
                         b * self.nlb + (s - self.ncb))


def _mm_body(x_ref, w_ref, b_ref, o_ref, wb_ref, *, act):
    @pl.when(pl.program_id(1) == 0)
    def _():
        wb_ref[...] = w_ref[...].astype(BF16)

    acc = jnp.dot(x_ref[...].astype(BF16), wb_ref[...], preferred_element_type=F32) + b_ref[...]
    if act == "tanh":
        acc = jnp.tanh(acc)
    elif act == "sigmoid":
        acc = jax.nn.sigmoid(acc)
    o_ref[...] = acc.astype(o_ref.dtype)


MM_ROW_TILES = (1536, 1024, 512, 256, 128, 64, 32, 16, 8)
MM_COL_TILES = (1024, 512, 256, 128)


def _mm(x, w, b=None, act=None, out_dtype=F32):
    m, k = x.shape
    n = w.shape[1]
    tm = next(t for t in MM_ROW_TILES if m % t == 0)
    tn = next((t for t in MM_COL_TILES if n % t == 0), n)
    if b is None:
        b = jnp.zeros((n,), F32)
    return pl.pallas_call(
        functools.partial(_mm_body, act=act),
        grid=(n // tn, m // tm),
        in_specs=[
            pl.BlockSpec((tm, k), lambda j, i: (i, 0)),
            pl.BlockSpec((k, tn), lambda j, i: (0, j)),
            pl.BlockSpec((1, tn), lambda j, i: (0, j)),
        ],
        out_specs=pl.BlockSpec((tm, tn), lambda j, i: (i, j)),
        out_shape=jax.ShapeDtypeStruct((m, n), out_dtype),
        scratch_shapes=[pltpu.VMEM((k, tn), BF16)],
        compiler_params=_params(("arbitrary", "arbitrary")),
        name="mm",
    )(x, w, b.reshape(1, n).astype(F32))


def _mm_exact_body(x_ref, w_ref, b_ref, o_ref):
    acc = jnp.dot(x_ref[...], w_ref[...], precision=HIGHEST, preferred_element_type=F32)
    o_ref[...] = acc + b_ref[...]


def _mm_exact(x, w, b, tn=512):
    m, k = x.shape
    n = w.shape[1]
    tn = min(tn, n)
    return pl.pallas_call(
        _mm_exact_body,
        grid=(n // tn,),
        in_specs=[
            pl.BlockSpec((m, k), lambda j: (0, 0)),
            pl.BlockSpec((k, tn), lambda j: (0, j)),
            pl.BlockSpec((1, tn), lambda j: (0, j)),
        ],
        out_specs=pl.BlockSpec((m, tn), lambda j: (0, j)),
        out_shape=jax.ShapeDtypeStruct((m, n), F32),
        compiler_params=_params(("arbitrary",)),
        name="mm_exact",
    )(x, w, b.reshape(1, n))


def _norm_mod_body(x_ref, g_ref, m_ref, o_ref):
    o_ref[...] = _norm_modulate(x_ref[...], g_ref[...], m_ref[0, 0:1], m_ref[0, 1:2]).astype(o_ref.dtype)


def _norm_mod(xs, g, modt, lay, out_dtype):
    m, d = xs.shape
    return pl.pallas_call(
        _norm_mod_body,
        grid=(m // ROW_BLOCK,),
        in_specs=[
            pl.BlockSpec((ROW_BLOCK, d), lambda i: (i, 0)),
            pl.BlockSpec((1, d), lambda i: (0, 0)),
            pl.BlockSpec((1,) + modt.shape[1:], lambda i: (lay.segment(i), 0, 0)),
        ],
        out_specs=pl.BlockSpec((ROW_BLOCK, d), lambda i: (i, 0)),
        out_shape=jax.ShapeDtypeStruct((m, d), out_dtype),
        compiler_params=_params(("arbitrary",)),
        name="norm_mod",
    )(xs, g.reshape(1, d), modt)


HG_HB = 4


def _gla_chunk(q, f, v, lb, sts, rev, consts):
    tri, ones_kk, trow = consts
    heads = range(len(sts))
    hs = [slice(h * HG_DK, (h + 1) * HG_DK) for h in heads]
    q = q * jax.nn.sigmoid(q)
    g = jnp.log(lb + (1.0 - lb) * jax.nn.sigmoid(f))
    k = (1.0 - lb) * jax.nn.sigmoid(-f)
    b = jnp.dot(tri, g, precision=HIGHEST, preferred_element_type=F32)
    b_end = b[0:1] if rev else b[CHUNK - 1:CHUNK]
    q_in = q * jnp.exp(b)
    k_end = k * jnp.exp(b_end - b)
    o_inter = [_dot_nt(q_in[:, s], sts[h]) for h, s in zip(heads, hs)]
    kv = [_dot(v[:, s].T, k_end[:, s]) for s in hs]
    e_end = jnp.exp(b_end)
    sts_new = [sts[h] * e_end[:, hs[h]] + kv[h] for h in heads]

    outs = [None] * (CHUNK // SUB)
    for p in range(CHUNK // SUB):
        if rev:
            r0 = CHUNK - SUB * (p + 1)
            e0, e1, mrow = CHUNK - SUB * p, CHUNK, CHUNK - SUB * p
        else:
            r0 = SUB * p
            e0, e1, mrow = 0, SUB * p, SUB * p - 1
        qp, bp, kp, vp = (x[r0:r0 + SUB] for x in (q, b, k, v))
        acc = [o_inter[h][r0:r0 + SUB] for h in heads]
        if p > 0:
            m = b[mrow:mrow + 1]
            q_lo = qp * jnp.exp(bp - m)
            k_hi = k[e0:e1] * jnp.exp(m - b[e0:e1])
            sc = [_dot_nt(q_lo[:, s], k_hi[:, s]) for s in hs]
            acc = [acc[h] + _dot(sc[h], v[e0:e1, hs[h]]) for h in heads]
        terms = []
        for s in range(SUB):
            mask = (trow <= s) if rev else (trow >= s)
            w = jnp.exp(jnp.where(mask, bp - bp[s:s + 1], -1e30))
            terms.append((qp * kp[s:s + 1] * w).astype(BF16))
        terms = jnp.concatenate(terms, axis=0)
        sums = [jnp.dot(terms[:, s], ones_kk, preferred_element_type=F32) for s in hs]
        for s in range(SUB):
            acc = [acc[h] + sums[h][s * SUB:(s + 1) * SUB] * vp[s:s + 1, hs[h]] for h in heads]
        outs[r0 // SUB] = jnp.concatenate(acc, axis=1)
    return jnp.concatenate(outs, axis=0), sts_new


def _gla_body(q_ref, f_ref, v_ref, lb_ref, o_ref, st_ref, *, rev):
    @pl.when(pl.program_id(2) == 0)
    def _():
        st_ref[...] = jnp.zeros_like(st_ref)

    row = lax.broadcasted_iota(jnp.int32, (CHUNK, CHUNK), 0)
    col = lax.broadcasted_iota(jnp.int32, (CHUNK, CHUNK), 1)
    tri = ((row <= col) if rev else (row >= col)).astype(F32)
    ones_kk = jnp.ones((HG_DK, HG_DK), BF16)
    trow = lax.broadcasted_iota(jnp.int32, (SUB, HG_HB * HG_DK), 0)
    lb = lb_ref[0]
    n_chunks = ROW_BLOCK // CHUNK

    def step(i, carry):
        cc = (n_chunks - 1 - i) if rev else i
        sl = pl.ds(pl.multiple_of(cc * CHUNK, CHUNK), CHUNK)
        o, sts = _gla_chunk(q_ref[sl, :].astype(F32), f_ref[sl, :].astype(F32),
                            v_ref[sl, :].astype(F32), lb,
                            [st_ref[h] for h in range(HG_HB)], rev, (tri, ones_kk, trow))
        o_ref[sl, :] = o
        for h in range(HG_HB):
            st_ref[h] = sts[h]
        return carry

    lax.fori_loop(0, n_chunks, step, 0)


def _gla_scan(z, lb, lay, rev):
    m = z.shape[0]
    d_model = HG_HEADS * HG_DK
    ng = HG_HEADS // HG_HB
    width = HG_HB * HG_DK
    f_off = ng * (2 if rev else 1)
    v_off = ng * 3
    rmap = lambda b, c: lay.seq_block(b, c, rev)
    return pl.pallas_call(
        functools.partial(_gla_body, rev=rev),
        grid=(lay.bsz, ng, lay.nlb + lay.ncb),
        in_specs=[
            pl.BlockSpec((ROW_BLOCK, width), lambda b, h, c: (rmap(b, c), h)),
            pl.BlockSpec((ROW_BLOCK, width), lambda b, h, c: (rmap(b, c), f_off + h)),
            pl.BlockSpec((ROW_BLOCK, width), lambda b, h, c: (rmap(b, c), v_off + h)),
            pl.BlockSpec((1, 1, width), lambda b, h, c: (h, 0, 0)),
        ],
        out_specs=pl.BlockSpec((ROW_BLOCK, width), lambda b, h, c: (rmap(b, c), h)),
        out_shape=jax.ShapeDtypeStruct((m, d_model), F32),
        scratch_shapes=[pltpu.VMEM((HG_HB, HG_DK, HG_DK), F32)],
        compiler_params=_params(("arbitrary", "arbitrary", "arbitrary")),
        name="gla_rev" if rev else "gla_fwd",
    )(z, z, z, lb.reshape(ng, 1, width))


def _hg_readout_body(of_ref, ob_ref, g_ref, gw_ref, w_ref, xs_ref, m_ref, ng_ref,
                     xo_ref, ho_ref, wb_ref):
    @pl.when(pl.program_id(0) == 0)
    def _():
        wb_ref[...] = w_ref[...].astype(BF16)

    o = of_ref[...] + ob_ref[...]
    parts = []
    for h in range(HG_HEADS):
        blk = o[:, h * HG_DK:(h + 1) * HG_DK]
        ms = jnp.mean(blk * blk, axis=-1, keepdims=True)
        parts.append(blk * lax.rsqrt(ms + NORM_EPS))
    gate = g_ref[...].astype(F32)
    y = jnp.concatenate(parts, axis=1) * gw_ref[...] * (gate * jax.nn.sigmoid(gate))
    acc = jnp.dot(y.astype(BF16), wb_ref[...], preferred_element_type=F32)
    xs_new = xs_ref[...] + m_ref[0, 2:3] * acc
    xo_ref[...] = xs_new
    ho_ref[...] = _norm_modulate(xs_new, ng_ref[...], m_ref[0, 3:4], m_ref[0, 4:5])


def _hg_readout(o_f, o_b, z, gnorm_w, w_out, xs, modt, norm_g, lay):
    m, d = xs.shape
    row = pl.BlockSpec((ROW_BLOCK, d), lambda i: (i, 0))
    vec = pl.BlockSpec((1, d), lambda i: (0, 0))
    return pl.pallas_call(
        _hg_readout_body,
        grid=(m // ROW_BLOCK,),
        in_specs=[row, row,
                  pl.BlockSpec((ROW_BLOCK, d), lambda i: (i, 4)),
                  vec,
                  pl.BlockSpec((d, d), lambda i: (0, 0)),
                  row,
                  pl.BlockSpec((1,) + modt.shape[1:], lambda i: (lay.segment(i), 0, 0)),
                  vec],
        out_specs=[row, row],
        out_shape=[jax.ShapeDtypeStruct((m, d), F32), jax.ShapeDtypeStruct((m, d), F32)],
        scratch_shapes=[pltpu.VMEM((d, d), BF16)],
        compiler_params=_params(("arbitrary",)),
        name="hg_readout",
    )(o_f, o_b, z, jnp.tile(gnorm_w, HG_HEADS).reshape(1, d), w_out, xs, modt,
      norm_g.reshape(1, d))


RW_HB = 16


def _rwkv_chunk(lw, cum, r, k, v, kap, a, sts, rev, consts):
    incl, strict, eye = consts
    nh = len(sts)
    heads = range(nh)
    hs = [slice(h * RW_DK, (h + 1) * RW_DK) for h in heads]
    cum_ex = cum - lw
    tot = cum[0:1] if rev else cum[CHUNK - 1:CHUNK]
    e_neg = jnp.exp(-cum)
    kt = kap * jnp.exp(cum_ex)
    bb = kap * a
    bt = bb * e_neg
    kd = k * e_neg
    rt = r * jnp.exp(cum)
    dec_end = jnp.exp(tot - cum)
    b_end = bb * dec_end
    k_end = k * dec_end
    e_tot = jnp.exp(tot)

    gram = [_dot_nt(jnp.concatenate([kt[:, s], rt[:, s]], axis=0),
                    jnp.concatenate([bt[:, s], kd[:, s]], axis=0)) for s in hs]
    a_bk = [jnp.where(strict, g[:CHUNK, :CHUNK], 0.0) for g in gram]
    a_kr = [jnp.concatenate([jnp.where(strict, g[:CHUNK, CHUNK:], 0.0),
                             jnp.where(incl, g[CHUNK:, CHUNK:], 0.0)], axis=0) for g in gram]
    a_rb = [jnp.where(incl, g[CHUNK:, :CHUNK], 0.0) for g in gram]
    av = [_dot(a_kr[h], v[:, hs[h]]) for h in heads]

    tinv = [eye - x for x in a_bk]
    pw = [_dot(x, x) for x in a_bk]
    for _ in range(4):
        tp = [_dot(jnp.concatenate([tinv[h], pw[h]], axis=0), pw[h]) for h in heads]
        tinv = [tinv[h] + tp[h][:CHUNK] for h in heads]
        pw = [x[CHUNK:] for x in tp]
    tinv = [tinv[h] + _dot(tinv[h], pw[h]) for h in heads]

    ku = [_dot(tinv[h], jnp.concatenate([kt[:, hs[h]], av[h][:CHUNK]], axis=1)) for h in heads]
    ab = [_dot(a_rb[h], ku[h]) for h in heads]
    rh = [rt[:, hs[h]] - ab[h][:, :RW_DK] for h in heads]
    y0 = [av[h][CHUNK:] - ab[h][:, RW_DK:] for h in heads]
    ys = [_dot_nt(rh[h], sts[h]) + y0[h] for h in heads]
    mb = [_dot(ku[h].T, b_end[:, hs[h]]) for h in heads]
    m_mat = [jnp.where(eye > 0, e_tot[:, hs[h]], 0.0) - mb[h][:RW_DK] for h in heads]
    n_mat = [_dot(v[:, hs[h]].T, k_end[:, hs[h]]) - mb[h][RW_DK:] for h in heads]
    sts_new = [_dot3(sts[h], m_mat[h]) + n_mat[h] for h in heads]
    return jnp.concatenate(ys, axis=1), sts_new


def _rwkv_body(r_ref, k_ref, v_ref, lw_ref, la_ref, w2_ref, w0_ref, a2_ref, a0_ref,
               kk_ref, ka_ref, rk_ref, y_ref, bv_ref, st_ref, *, rev):
    @pl.when(pl.program_id(2) == 0)
    def _():
        st_ref[...] = jnp.zeros_like(st_ref)

    row = lax.broadcasted_iota(jnp.int32, (CHUNK, CHUNK), 0)
    col = lax.broadcasted_iota(jnp.int32, (CHUNK, CHUNK), 1)
    incl = (row <= col) if rev else (row >= col)
    strict = (row < col) if rev else (row > col)
    eye = (row == col).astype(F32)
    tri = incl.astype(F32)
    ones_blk = _head_ones(RW_DK)
    n_chunks = ROW_BLOCK // CHUNK

    def step(i, carry):
        cc = (n_chunks - 1 - i) if rev else i
        sl = pl.ds(pl.multiple_of(cc * CHUNK, CHUNK), CHUNK)
        r, k, v = (ref[sl, :].astype(F32) for ref in (r_ref, k_ref, v_ref))
        lo_w = (1 if rev else 0) * (lw_ref.shape[1] // 2)
        lo_a = (1 if rev else 0) * (la_ref.shape[1] // 2)
        zw = jnp.dot(lw_ref[sl, :][:, lo_w:lo_w + w2_ref.shape[0]], w2_ref[...].astype(BF16),
                     preferred_element_type=F32) + w0_ref[...]
        za = jnp.dot(la_ref[sl, :][:, lo_a:lo_a + a2_ref.shape[0]], a2_ref[...].astype(BF16),
                     preferred_element_type=F32) + a0_ref[...]
        kmul = k * kk_ref[...]
        kap = kmul * lax.rsqrt(jnp.maximum(_head_sums(kmul * kmul, ones_blk), 1e-24))
        a = jax.nn.sigmoid(za)
        lw = -jnp.exp(-jax.nn.softplus(-zw) - 0.5)
        k_dir = k * (1.0 + (a - 1.0) * ka_ref[...])
        cum = jnp.dot(tri, lw, precision=HIGHEST, preferred_element_type=F32)
        y, sts = _rwkv_chunk(lw, cum, r, k_dir, v, kap, a, [st_ref[h] for h in range(RW_HB)],
                             rev, (incl, strict, eye))
        for h in range(RW_HB):
            st_ref[h] = sts[h]
        y_ref[sl, :] = y
        bv_ref[sl, :] = _head_sums(r * k_dir * rk_ref[...], ones_blk) * v
        return carry

    lax.fori_loop(0, n_chunks, step, 0)


def _rwkv_scan(r, k, v, lora_w, lora_a, w2, w0, a2, a0, k_k, k_a, r_k, lay, rev):
    m, d_model = r.shape
    width = RW_HB * RW_DK
    rmap = lambda b, c: lay.seq_block(b, c, rev)
    spec = pl.BlockSpec((ROW_BLOCK, width), lambda b, h, c: (rmap(b, c), h))
    lora = lambda x: pl.BlockSpec((ROW_BLOCK, x.shape[1]), lambda b, h, c: (rmap(b, c), 0))
    mat = lambda x: pl.BlockSpec((x.shape[0], width), lambda b, h, c: (0, h))
    vec = pl.BlockSpec((1, width), lambda b, h, c: (0, h))
    out = jax.ShapeDtypeStruct((m, d_model), F32)
    row = lambda x: x.reshape(1, d_model)
    return pl.pallas_call(
        functools.partial(_rwkv_body, rev=rev),
        grid=(lay.bsz, d_model // width, lay.nlb + lay.ncb),
        in_specs=[spec] * 3 + [lora(lora_w), lora(lora_a), mat(w2), vec, mat(a2), vec] + [vec] * 3,
        out_specs=[spec, spec],
        out_shape=[out, out],
        scratch_shapes=[pltpu.VMEM((RW_HB, RW_DK, RW_DK), F32)],
        compiler_params=_params(("arbitrary", "arbitrary", "arbitrary")),
        name="rwkv_rev" if rev else "rwkv_fwd",
    )(r, k, v, lora_w, lora_a, w2, row(w0), a2, row(a0), row(k_k), row(k_a), row(r_k))


def _rw_readout_body(yf_ref, yb_ref, bf_ref, bb_ref, g_ref, lw_ref, lb_ref, w_ref, xs_ref,
                     m_ref, ng_ref, xo_ref, ho_ref, wb_ref):
    @pl.when(pl.program_id(0) == 0)
    def _():
        wb_ref[...] = w_ref[...].astype(BF16)

    ones_blk = _head_ones(RW_DK)
    y = yf_ref[...] + yb_ref[...]
    dev = y - _head_sums(y, ones_blk) * (1.0 / RW_DK)
    var = _head_sums(dev * dev, ones_blk) * (1.0 / RW_DK)
    yn = dev * lax.rsqrt(var + RW_LNX_EPS) * lw_ref[...] + lb_ref[...]
    yn = yn + bf_ref[...] + bb_ref[...]
    acc = jnp.dot((yn * g_ref[...].astype(F32)).astype(BF16), wb_ref[...],
                  preferred_element_type=F32)
    xs_new = xs_ref[...] + m_ref[0, 2:3] * acc
    xo_ref[...] = xs_new
    ho_ref[...] = _norm_modulate(xs_new, ng_ref[...], m_ref[0, 3:4], m_ref[0, 4:5])


def _rw_readout(y_f, y_b, bv_f, bv_b, gate, lnx_w, lnx_b, w_out, xs, modt, norm_g, lay):
    m, d = xs.shape
    row = pl.BlockSpec((ROW_BLOCK, d), lambda i: (i, 0))
    vec = pl.BlockSpec((1, d), lambda i: (0, 0))
    return pl.pallas_call(
        _rw_readout_body,
        grid=(m // ROW_BLOCK,),
        in_specs=[row] * 5 + [vec, vec, pl.BlockSpec((d, d), lambda i: (0, 0)), row,
                              pl.BlockSpec((1,) + modt.shape[1:],
                                           lambda i: (lay.segment(i), 0, 0)),
                              vec],
        out_specs=[row, row],
        out_shape=[jax.ShapeDtypeStruct((m, d), F32), jax.ShapeDtypeStruct((m, d), F32)],
        scratch_shapes=[pltpu.VMEM((d, d), BF16)],
        compiler_params=_params(("arbitrary",)),
        name="rw_readout",
    )(y_f, y_b, bv_f, bv_b, gate, lnx_w.reshape(1, d), lnx_b.reshape(1, d), w_out, xs, modt,
      norm_g.reshape(1, d))


def _router_body(x_ref, w_ref, b_ref, idx_ref, gate_ref):
    logits = _dot3(x_ref[...], w_ref[...]) + b_ref[...]
    lane = lax.broadcasted_iota(jnp.int32, logits.shape, 1)
    vals, idxs = [], []
    for _ in range(TOP_K):
        m = jnp.max(logits, axis=-1, keepdims=True)
        idx = jnp.min(jnp.where(logits == m, lane, N_EXPERTS), axis=-1, keepdims=True)
        vals.append(m)
        idxs.append(idx)
        logits = jnp.where(lane == idx, -jnp.inf, logits)
    exps = [jnp.exp(val - vals[0]) for val in vals]
    denom = exps[0] + exps[1] + exps[2] + exps[3]
    idx_ref[...] = jnp.concatenate(idxs, axis=1)
    gate_ref[...] = jnp.concatenate(exps, axis=1) / denom


def _router(x, n, w, b):
    d = x.shape[1]
    return pl.pallas_call(
        _router_body,
        grid=(n // ROW_BLOCK,),
        in_specs=[
            pl.BlockSpec((ROW_BLOCK, d), lambda i: (i, 0)),
            pl.BlockSpec((d, N_EXPERTS), lambda i: (0, 0)),
            pl.BlockSpec((1, N_EXPERTS), lambda i: (0, 0)),
        ],
        out_specs=[pl.BlockSpec((ROW_BLOCK, TOP_K), lambda i: (i, 0)),
                   pl.BlockSpec((ROW_BLOCK, TOP_K), lambda i: (i, 0))],
        out_shape=[jax.ShapeDtypeStruct((n, TOP_K), jnp.int32),
                   jax.ShapeDtypeStruct((n, TOP_K), F32)],
        compiler_params=_params(("arbitrary",)),
        name="router",
    )(x, w, b.reshape(1, N_EXPERTS))


EXPERT_COLS = 512


def _expert_body(be_ref, nu_ref, x_ref, wgu_ref, bgu_ref, wd_ref, bd_ref, o_ref,
                 wgu_b, wd_b):
    i = pl.program_id(0)
    d_exp = wd_ref.shape[1]
    prev = be_ref[jnp.maximum(i - 1, 0)]

    @pl.when((i == 0) | (be_ref[i] != prev))
    def _():
        wgu_b[...] = wgu_ref[0].astype(BF16)
        wd_b[...] = wd_ref[0].astype(BF16)

    @pl.when(i < nu_ref[0])
    def _():
        x = x_ref[...].astype(BF16)
        acc = jnp.zeros(o_ref.shape, F32)
        for c0 in range(0, d_exp, EXPERT_COLS):
            gate = jnp.dot(x, wgu_b[:, c0:c0 + EXPERT_COLS], preferred_element_type=F32)
            gate = gate + bgu_ref[0, :, c0:c0 + EXPERT_COLS]
            up = jnp.dot(x, wgu_b[:, d_exp + c0:d_exp + c0 + EXPERT_COLS],
                         preferred_element_type=F32)
            up = up + bgu_ref[0, :, d_exp + c0:d_exp + c0 + EXPERT_COLS]
            gate = jnp.minimum(gate, SWIGLU_LIMIT)
            up = jnp.clip(up, -SWIGLU_LIMIT, SWIGLU_LIMIT)
            act = gate * jax.nn.sigmoid(SWIGLU_ALPHA * gate) * (up + 1.0)
            acc = acc + jnp.dot(act.astype(BF16), wd_b[c0:c0 + EXPERT_COLS, :],
                                preferred_element_type=F32)
        o_ref[...] = acc + bd_ref[0]

    @pl.when(i >= nu_ref[0])
    def _():
        o_ref[...] = jnp.zeros_like(o_ref)


def _experts(xs, block_e, n_used, w_gu, b_gu, w_down, b_down):
    cap, d = xs.shape
    n_e, _, d2 = w_gu.shape
    d_exp = d2 // 2
    nb = cap // MOE_BLOCK
    grid_spec = pltpu.PrefetchScalarGridSpec(
        num_scalar_prefetch=2,
        grid=(nb,),
        in_specs=[
            pl.BlockSpec((MOE_BLOCK, d), lambda i, be, nu: (i, 0)),
            pl.BlockSpec((1, d, d2), lambda i, be, nu: (be[i], 0, 0)),
            pl.BlockSpec((1, 1, d2), lambda i, be, nu: (be[i], 0, 0)),
            pl.BlockSpec((1, d_exp, d), lambda i, be, nu: (be[i], 0, 0)),
            pl.BlockSpec((1, 1, d), lambda i, be, nu: (be[i], 0, 0)),
        ],
        out_specs=pl.BlockSpec((MOE_BLOCK, d), lambda i, be, nu: (i, 0)),
        scratch_shapes=[pltpu.VMEM((d, d2), BF16), pltpu.VMEM((d_exp, d), BF16)],
    )
    return pl.pallas_call(
        _expert_body,
        grid_spec=grid_spec,
        out_shape=jax.ShapeDtypeStruct((cap, d), F32),
        compiler_params=_params(("arbitrary",)),
        name="experts",
    )(block_e, n_used, xs, w_gu, b_gu.reshape(n_e, 1, d2), w_down, b_down.reshape(n_e, 1, d))


def _row_copies(make_copy, start):
    def body(r, carry):
        for j in range(TOP_K):
            cp = make_copy(r, j)
            cp.start(priority=j % 2) if start else cp.wait()
        return carry

    lax.fori_loop(0, ROW_BLOCK, body, 0, unroll=8)


def _dispatch_body(pad_ref, dst_ref, x_ref, xs_ref, zeros, sem, zsem):
    @pl.when(pl.program_id(0) == 0)
    def _():
        zeros[...] = jnp.zeros_like(zeros)
        for e in range(N_EXPERTS):
            start = pl.multiple_of(pad_ref[e], SUBLANES)
            fill = pltpu.make_async_copy(zeros, xs_ref.at[pl.ds(start, PAD_FILL)], zsem)
            fill.start()
            fill.wait()

    def copy(r, j):
        return pltpu.make_async_copy(x_ref.at[pl.ds(r, 1)],
                                     xs_ref.at[pl.ds(dst_ref[0, 0, r * TOP_K + j], 1)], sem)

    _row_copies(copy, True)
    _row_copies(copy, False)


def _dispatch(tokens, n, dest, pad_start, cap):
    d = tokens.shape[1]
    nblk = n // ROW_BLOCK
    grid_spec = pltpu.PrefetchScalarGridSpec(
        num_scalar_prefetch=1,
        grid=(nblk,),
        in_specs=[
            pl.BlockSpec((1, 1, ROW_BLOCK * TOP_K), lambda i, pad: (i, 0, 0),
                         memory_space=pltpu.SMEM),
            pl.BlockSpec((ROW_BLOCK, d), lambda i, pad: (i, 0)),
        ],
        out_specs=pl.BlockSpec(memory_space=pl.ANY),
        scratch_shapes=[pltpu.VMEM((PAD_FILL, d), F32), pltpu.SemaphoreType.DMA(()),
                        pltpu.SemaphoreType.DMA(())],
    )
    return pl.pallas_call(
        _dispatch_body,
        grid_spec=grid_spec,
        out_shape=jax.ShapeDtypeStruct((cap, d), F32),
        compiler_params=_params(("arbitrary",)),
        name="dispatch",
    )(pad_start, dest.reshape(nblk, 1, ROW_BLOCK * TOP_K), tokens)


def _combine_body(dst_ref, gate_ref, xs_ref, m_ref, ng_ref, m2_ref, ys_ref, xo_ref, ho_ref,
                  buf, sem):
    def copy(r, j):
        return pltpu.make_async_copy(ys_ref.at[pl.ds(dst_ref[0, 0, r * TOP_K + j], 1)],
                                     buf.at[j, pl.ds(r, 1)], sem)

    _row_copies(copy, True)
    _row_copies(copy, False)
    gates = gate_ref[...]
    acc = gates[:, 0:1] * buf[0]
    for j in range(1, TOP_K):
        acc = acc + gates[:, j:j + 1] * buf[j]
    xs_new = xs_ref[...] + m_ref[0, 5:6] * acc
    xo_ref[...] = xs_new
    ho_ref[...] = _norm_modulate(xs_new, ng_ref[...], m2_ref[0, 0:1], m2_ref[0, 1:2])


def _combine(ys, dest, gates, xs, modt, next_g, next_modt, lay):
    n = gates.shape[0]
    d = ys.shape[1]
    nblk = n // ROW_BLOCK
    row = pl.BlockSpec((ROW_BLOCK, d), lambda i: (i, 0))
    seg = lambda mt: pl.BlockSpec((1,) + mt.shape[1:], lambda i: (lay.segment(i), 0, 0))
    return pl.pallas_call(
        _combine_body,
        grid=(nblk,),
        in_specs=[
            pl.BlockSpec((1, 1, ROW_BLOCK * TOP_K), lambda i: (i, 0, 0),
                         memory_space=pltpu.SMEM),
            pl.BlockSpec((ROW_BLOCK, TOP_K), lambda i: (i, 0)),
            row, seg(modt), pl.BlockSpec((1, d), lambda i: (0, 0)), seg(next_modt),
            pl.BlockSpec(memory_space=pl.ANY),
        ],
        out_specs=[row, row],
        out_shape=[jax.ShapeDtypeStruct((n, d), F32), jax.ShapeDtypeStruct((n, d), F32)],
        scratch_shapes=[pltpu.VMEM((TOP_K, ROW_BLOCK, d), F32), pltpu.SemaphoreType.DMA(())],
        compiler_params=_params(("arbitrary",)),
        name="combine",
    )(dest.reshape(nblk, 1, ROW_BLOCK * TOP_K), gates, xs, modt, next_g.reshape(1, d),
      next_modt, ys)


def _moe(h, n, xs, modt, next_g, next_modt, lay, router_w, router_b, w_gu, b_gu, w_down, b_down):
    d = h.shape[1]
    top_idx, gates = _router(h, n, router_w, router_b)
    flat_e = top_idx.reshape(-1)
    onehot = (flat_e[:, None] == jnp.arange(N_EXPERTS, dtype=jnp.int32)[None, :]).astype(jnp.int32)
    rank = jnp.sum((jnp.cumsum(onehot, axis=0) - 1) * onehot, axis=1)
    counts = jnp.sum(onehot, axis=0)
    padded = (counts + MOE_BLOCK - 1) // MOE_BLOCK * MOE_BLOCK
    padded_end = jnp.cumsum(padded)
    padded_start = padded_end - padded
    dest = jnp.sum(onehot * padded_start[None, :], axis=1) + rank
    n_blocks = -(-(n * TOP_K + N_EXPERTS * (MOE_BLOCK - 1)) // MOE_BLOCK)
    cap = n_blocks * MOE_BLOCK
    n_used = (padded_end[-1] // MOE_BLOCK).astype(jnp.int32)
    blk = jnp.minimum(jnp.arange(n_blocks, dtype=jnp.int32), n_used - 1) * MOE_BLOCK
    block_e = jnp.minimum(jnp.sum((padded_end[None, :] <= blk[:, None]).astype(jnp.int32), axis=1),
                          N_EXPERTS - 1)
    pad_start = jnp.minimum((padded_start + counts) // SUBLANES * SUBLANES,
                            cap - PAD_FILL).astype(jnp.int32)
    slots = _dispatch(h, n, dest, pad_start, cap)
    ys = _experts(slots, block_e, n_used.reshape(1), w_gu, b_gu, w_down, b_down)
    return _combine(ys, dest, gates, xs, modt, next_g, next_modt, lay)


def _shift_lerp_body(hp_ref, h_ref, hn_ref, mu_ref, *o_refs, bsz, nlb, ncb):
    i = pl.program_id(0)
    x = h_ref[...]
    d = x.shape[1]
    q, half, last = d // 4, d // 2, ROW_BLOCK - 1
    row = lax.broadcasted_iota(jnp.int32, (ROW_BLOCK, 1), 0)
    col = row % GRID_W
    lat_blk = i % nlb
    left = jnp.where(col == 0, 0.0, pltpu.roll(x[:, :q], 1, 0))
    right = jnp.where(col == GRID_W - 1, 0.0, pltpu.roll(x[:, q:2 * q], last, 0))
    up = jnp.concatenate([hp_ref[ROW_BLOCK - GRID_W:, 2 * q:3 * q],
                          x[:ROW_BLOCK - GRID_W, 2 * q:3 * q]], axis=0)
    up = jnp.where((row < GRID_W) & (lat_blk == 0), 0.0, up)
    down = jnp.concatenate([x[GRID_W:, 3 * q:], hn_ref[:GRID_W, 3 * q:]], axis=0)
    down = jnp.where((row >= ROW_BLOCK - GRID_W) & (lat_blk == nlb - 1), 0.0, down)
    shift_lat = jnp.concatenate([left, right, up, down], axis=1)

    ctx_blk = (i - bsz * nlb) % ncb
    prev_edge = jnp.where(ctx_blk == 0, 0.0, hp_ref[last:, :half])
    prev = jnp.where(row == 0, prev_edge, pltpu.roll(x[:, :half], 1, 0))
    next_edge = jnp.where(ctx_blk == ncb - 1, 0.0, hn_ref[0:1, half:])
    nxt = jnp.where(row == last, next_edge, pltpu.roll(x[:, half:], last, 0))
    shift_ctx = jnp.concatenate([prev, nxt], axis=1)

    dx = jnp.where(i < bsz * nlb, shift_lat, shift_ctx) - x
    for j, o_ref in enumerate(o_refs):
        o_ref[...] = (x + dx * mu_ref[j:j + 1]).astype(o_ref.dtype)


def _shift_lerp(h, mu, lay):
    assert ROW_BLOCK % GRID_W == 0 and lay.n_lat % GRID_W == 0
    m, d = h.shape
    nblk = m // ROW_BLOCK
    n_out = mu.shape[0]
    row = pl.BlockSpec((ROW_BLOCK, d), lambda i: (i, 0))
    return pl.pallas_call(
        functools.partial(_shift_lerp_body, bsz=lay.bsz, nlb=lay.nlb, ncb=lay.ncb),
        grid=(nblk,),
        in_specs=[pl.BlockSpec((ROW_BLOCK, d), lambda i: (jnp.maximum(i - 1, 0), 0)),
                  row,
                  pl.BlockSpec((ROW_BLOCK, d), lambda i: (jnp.minimum(i + 1, nblk - 1), 0)),
                  pl.BlockSpec((n_out, d), lambda i: (0, 0))],
        out_specs=[row] * n_out,
        out_shape=[jax.ShapeDtypeStruct((m, d), BF16)] * n_out,
        compiler_params=_params(("arbitrary",)),
        name="shift_lerp",
    )(h, h, h, mu)


def _rwkv7_mixer(h, xs, modt, norm_g, lay, mu, w_rkv, dec_w0, dec_w1, dec_w2, iclr_a0, iclr_a1,
                 iclr_a2, g1, g2, k_k, k_a, r_k, lnx_w, lnx_b, w_out):
    xr, xw, xk, xv, xa, xg = _shift_lerp(h, mu, lay)
    r = _mm(xr, w_rkv[0], out_dtype=BF16)
    k = _mm(xk, w_rkv[1], out_dtype=BF16)
    v = _mm(xv, w_rkv[2], out_dtype=BF16)
    lora_w = _mm(xw, jnp.concatenate([dec_w1[0], dec_w1[1]], axis=1), act="tanh", out_dtype=BF16)
    lora_a = _mm(xa, jnp.concatenate([iclr_a1[0], iclr_a1[1]], axis=1), out_dtype=BF16)
    gate = _mm(_mm(xg, g1, act="sigmoid", out_dtype=BF16), g2, out_dtype=BF16)
    ys = []
    for dd, rev in enumerate((False, True)):
        ys.append(_rwkv_scan(r, k, v, lora_w, lora_a, dec_w2[dd], dec_w0[dd], iclr_a2[dd],
                             iclr_a0[dd], k_k, k_a, r_k, lay, rev))
    return _rw_readout(ys[0][0], ys[1][0], ys[0][1], ys[1][1], gate, lnx_w, lnx_b, w_out,
                       xs, modt, norm_g, lay)


def kernel(x, c, ctx, c_ctx, ada_w, ada_b, norm_mix_g, norm_ffn_g, hg_w_in, hg_gnorm_w, hg_w_out, hg_lb, rw_mu, rw_w_rkv, rw_dec_w0, rw_dec_w1, rw_dec_w2, rw_iclr_a0, rw_iclr_a1, rw_iclr_a2, rw_g1, rw_g2, rw_k_k, rw_k_a, rw_r_k, rw_lnx_w, rw_lnx_b, rw_w_out, moe_router_w, moe_router_b, moe_w_gu, moe_b_gu, moe_w_down, moe_b_down, final_g):
    bsz, n_lat, d = x.shape
    n_ctx = ctx.shape[1]
    depth = ada_w.shape[0]
    lay = _Layout(bsz, n_lat, n_ctx)
    lb_all = jnp.cumsum(jax.nn.softmax(hg_lb.astype(F32), axis=1), axis=1)
    cond = jnp.concatenate([jax.nn.silu(c), jax.nn.silu(c_ctx)[None, :],
                            jnp.zeros((8 - bsz - 1, d), F32)], axis=0)
    modts = [_mm_exact(cond, ada_w[l], ada_b[l])[:bsz + 1].reshape(bsz + 1, 6, d)
             for l in range(depth)]
    xs = jnp.concatenate([x.reshape(bsz * n_lat, d), ctx.reshape(bsz * n_ctx, d)], axis=0)
    h = _norm_mod(xs, norm_mix_g[0], modts[0], lay, BF16)
    n_rows = lay.rows
    for layer in range(depth):
        last = layer == depth - 1
        modt = modts[layer]
        j = layer // 2
        if layer % 2 == 0:
            z = _mm(h.astype(BF16), hg_w_in[j], out_dtype=BF16)
            lb = lb_all[:, layer]
            o_f = _gla_scan(z, lb[0], lay, False)
            o_b = _gla_scan(z, lb[1], lay, True)
            xs, h = _hg_readout(o_f, o_b, z, hg_gnorm_w[j], hg_w_out[j], xs, modt,
                                norm_ffn_g[layer], lay)
        else:
            xs, h = _rwkv7_mixer(h, xs, modt, norm_ffn_g[layer], lay, rw_mu[j], rw_w_rkv[j],
                                 rw_dec_w0[j], rw_dec_w1[j], rw_dec_w2[j], rw_iclr_a0[j],
                                 rw_iclr_a1[j], rw_iclr_a2[j], rw_g1[j], rw_g2[j], rw_k_k[j],
                                 rw_k_a[j], rw_r_k[j], rw_lnx_w[j], rw_lnx_b[j], rw_w_out[j])
        if last:
            n_rows = bsz * n_lat
            next_g, next_modt = final_g, jnp.zeros_like(modt)
        else:
            next_g, next_modt = norm_mix_g[layer + 1], modts[layer + 1]
        xs, h = _moe(h, n_rows, xs, modt, next_g, next_modt, lay, moe_router_w[layer],
                     moe_router_b[layer], moe_w_gu[layer], moe_b_gu[layer], moe_w_down[layer],
                     moe_b_down[layer])
    return h.reshape(bsz, n_lat, d)
```

```python
import functools

import jax
import jax.numpy as jnp
from jax import lax
from jax.experimental import pallas as pl
from jax.experimental.pallas import tpu as pltpu

F32 = jnp.float32
BF16 = jnp.bfloat16
HIGHEST = lax.Precision.HIGHEST

GRID_W = 64
HG_HEADS = 8
HG_DK = 128
RW_HEADS = 16
RW_DK = 64
N_EXPERTS = 32
TOP_K = 4
SWIGLU_LIMIT = 7.0
SWIGLU_ALPHA = 1.702
NORM_EPS = 1e-6
RW_LNX_EPS = 1e-5 * RW_DK

LANES = 128
CHUNK = 64
SUB = 16
ROW_BLOCK = 256
MOE_BLOCK = 256
SUBLANES = 8
PAD_FILL = MOE_BLOCK + SUBLANES
VMEM_LIMIT = 56 * 1024 * 1024

_NT = (((1,), (1,)), ((), ()))


def _dot(a, b):
    return jnp.dot(a.astype(BF16), b.astype(BF16), preferred_element_type=F32)


def _dot_nt(a, b):
    return lax.dot_general(a.astype(BF16), b.astype(BF16), _NT, preferred_element_type=F32)


def _split(a):
    hi = a.astype(BF16)
    lo = (a - hi.astype(F32)).astype(BF16)
    return hi, lo


def _dot3(a, b):
    a_hi, a_lo = _split(a)
    b_hi, b_lo = _split(b)
    f = lambda x, y: jnp.dot(x, y, preferred_element_type=F32)
    return f(a_hi, b_hi) + (f(a_hi, b_lo) + f(a_lo, b_hi))


def _head_ones(width):
    r = lax.broadcasted_iota(jnp.int32, (LANES, LANES), 0) // width
    c = lax.broadcasted_iota(jnp.int32, (LANES, LANES), 1) // width
    return (r == c).astype(BF16)


def _head_sums(x, ones_blk):
    hi, lo = _split(x)
    outs = []
    for c in range(x.shape[1] // LANES):
        sl = slice(c * LANES, (c + 1) * LANES)
        outs.append(jnp.dot(hi[:, sl], ones_blk, preferred_element_type=F32)
                    + jnp.dot(lo[:, sl], ones_blk, preferred_element_type=F32))
    return jnp.concatenate(outs, axis=1)


def _norm_modulate(x, g, shift, scale):
    y = x * lax.rsqrt(jnp.mean(x * x, axis=-1, keepdims=True) + NORM_EPS) * g
    return y * (1.0 + scale) + shift


def _params(sem):
    return pltpu.CompilerParams(dimension_semantics=sem, vmem_limit_bytes=VMEM_LIMIT)


class _Layout:
    def __init__(self, bsz, n_lat, n_ctx):
        assert n_lat % ROW_BLOCK == 0 and n_ctx % ROW_BLOCK == 0
        self.bsz, self.n_lat, self.n_ctx = bsz, n_lat, n_ctx
        self.nlb = n_lat // ROW_BLOCK
        self.ncb = n_ctx // ROW_BLOCK
        self.rows = bsz * (n_lat + n_ctx)

    def segment(self, i):
        return jnp.minimum(i // self.nlb, self.bsz)

    def seq_block(self, b, c, rev):
        if rev:
            s = jnp.where(c < self.ncb, self.ncb - 1 - c, self.nlb + self.ncb - 1 - (c - self.ncb))
        else:
            s = c
        return jnp.where(s < self.ncb, self.bsz * self.nlb + b * self.ncb + s,
                         b * self.nlb + (s - self.ncb))


def _mm_body(x_ref, w_ref, b_ref, o_ref, wb_ref, *, act):
    @pl.when(pl.program_id(1) == 0)
    def _():
        wb_ref[...] = w_ref[...].astype(BF16)

    acc = jnp.dot(x_ref[...].astype(BF16), wb_ref[...], preferred_element_type=F32) + b_ref[...]
    if act == "tanh":
        acc = jnp.tanh(acc)
    elif act == "sigmoid":
        acc = jax.nn.sigmoid(acc)
    o_ref[...] = acc.astype(o_ref.dtype)


MM_ROW_TILES = (1536, 1024, 512, 256, 128, 64, 32, 16, 8)
MM_COL_TILES = (1024, 512, 256, 128)


def _mm(x, w, b=None, act=None, out_dtype=F32):
    m, k = x.shape
    n = w.shape[1]
    tm = next(t for t in MM_ROW_TILES if m % t == 0)
    tn = next((t for t in MM_COL_TILES if n % t == 0), n)
    if b is None:
        b = jnp.zeros((n,), F32)
    return pl.pallas_call(
        functools.partial(_mm_body, act=act),
        grid=(n // tn, m // tm),
        in_specs=[
            pl.BlockSpec((tm, k), lambda j, i: (i, 0)),
            pl.BlockSpec((k, tn), lambda j, i: (0, j)),
            pl.BlockSpec((1, tn), lambda j, i: (0, j)),
        ],
        out_specs=pl.BlockSpec((tm, tn), lambda j, i: (i, j)),
        out_shape=jax.ShapeDtypeStruct((m, n), out_dtype),
        scratch_shapes=[pltpu.VMEM((k, tn), BF16)],
        compiler_params=_params(("arbitrary", "arbitrary")),
        name="mm",
    )(x, w, b.reshape(1, n).astype(F32))


def _mm_exact_body(x_ref, w_ref, b_ref, o_ref):
    acc = jnp.dot(x_ref[...], w_ref[...], precision=HIGHEST, preferred_element_type=F32)
    o_ref[...] = acc + b_ref[...]


def _mm_exact(x, w, b, tn=512):
    m, k = x.shape
    n = w.shape[1]
    tn = min(tn, n)
    return pl.pallas_call(
        _mm_exact_body,
        grid=(n // tn,),
        in_specs=[
            pl.BlockSpec((m, k), lambda j: (0, 0)),
            pl.BlockSpec((k, tn), lambda j: (0, j)),
            pl.BlockSpec((1, tn), lambda j: (0, j)),
        ],
        out_specs=pl.BlockSpec((m, tn), lambda j: (0, j)),
        out_shape=jax.ShapeDtypeStruct((m, n), F32),
        compiler_params=_params(("arbitrary",)),
        name="mm_exact",
    )(x, w, b.reshape(1, n))


def _norm_mod_body(x_ref, g_ref, m_ref, o_ref):
    o_ref[...] = _norm_modulate(x_ref[...], g_ref[...], m_ref[0, 0:1], m_ref[0, 1:2]).astype(o_ref.dtype)


def _norm_mod(xs, g, modt, lay, out_dtype):
    m, d = xs.shape
    return pl.pallas_call(
        _norm_mod_body,
        grid=(m // ROW_BLOCK,),
        in_specs=[
            pl.BlockSpec((ROW_BLOCK, d), lambda i: (i, 0)),
            pl.BlockSpec((1, d), lambda i: (0, 0)),
            pl.BlockSpec((1,) + modt.shape[1:], lambda i: (lay.segment(i), 0, 0)),
        ],
        out_specs=pl.BlockSpec((ROW_BLOCK, d), lambda i: (i, 0)),
        out_shape=jax.ShapeDtypeStruct((m, d), out_dtype),
        compiler_params=_params(("arbitrary",)),
        name="norm_mod",
    )(xs, g.reshape(1, d), modt)


HG_HB = 8


def _gla_chunk(q, f, v, lb, sts, rev, consts):
    tri, ones_kk, trow = consts
    heads = range(len(sts))
    hs = [slice(h * HG_DK, (h + 1) * HG_DK) for h in heads]
    q = q * jax.nn.sigmoid(q)
    g = jnp.log(lb + (1.0 - lb) * jax.nn.sigmoid(f))
    k = (1.0 - lb) * jax.nn.sigmoid(-f)
    b = jnp.dot(tri, g, precision=HIGHEST, preferred_element_type=F32)
    b_end = b[0:1] if rev else b[CHUNK - 1:CHUNK]
    q_in = q * jnp.exp(b)
    k_end = k * jnp.exp(b_end - b)
    o_inter = [_dot_nt(q_in[:, s], sts[h]) for h, s in zip(heads, hs)]
    kv = [_dot(v[:, s].T, k_end[:, s]) for s in hs]
    e_end = jnp.exp(b_end)
    sts_new = [sts[h] * e_end[:, hs[h]] + kv[h] for h in heads]

    outs = [None] * (CHUNK // SUB)
    for p in range(CHUNK // SUB):
        if rev:
            r0 = CHUNK - SUB * (p + 1)
            e0, e1, mrow = CHUNK - SUB * p, CHUNK, CHUNK - SUB * p
        else:
            r0 = SUB * p
            e0, e1, mrow = 0, SUB * p, SUB * p - 1
        qp, bp, kp, vp = (x[r0:r0 + SUB] for x in (q, b, k, v))
        acc = [o_inter[h][r0:r0 + SUB] for h in heads]
        if p > 0:
            m = b[mrow:mrow + 1]
            q_lo = qp * jnp.exp(bp - m)
            k_hi = k[e0:e1] * jnp.exp(m - b[e0:e1])
            sc = [_dot_nt(q_lo[:, s], k_hi[:, s]) for s in hs]
            acc = [acc[h] + _dot(sc[h], v[e0:e1, hs[h]]) for h in heads]
        terms = []
        for s in range(SUB):
            mask = (trow <= s) if rev else (trow >= s)
            w = jnp.exp(jnp.where(mask, bp - bp[s:s + 1], -1e30))
            terms.append((qp * kp[s:s + 1] * w).astype(BF16))
        terms = jnp.concatenate(terms, axis=0)
        sums = [jnp.dot(terms[:, s], ones_kk, preferred_element_type=F32) for s in hs]
        for s in range(SUB):
            acc = [acc[h] + sums[h][s * SUB:(s + 1) * SUB] * vp[s:s + 1, hs[h]] for h in heads]
        outs[r0 // SUB] = jnp.concatenate(acc, axis=1)
    return jnp.concatenate(outs, axis=0), sts_new


def _gla_body(q_ref, f_ref, v_ref, lb_ref, o_ref, st_ref, *, rev):
    @pl.when(pl.program_id(2) == 0)
    def _():
        st_ref[...] = jnp.zeros_like(st_ref)

    row = lax.broadcasted_iota(jnp.int32, (CHUNK, CHUNK), 0)
    col = lax.broadcasted_iota(jnp.int32, (CHUNK, CHUNK), 1)
    tri = ((row <= col) if rev else (row >= col)).astype(F32)
    ones_kk = jnp.ones((HG_DK, HG_DK), BF16)
    trow = lax.broadcasted_iota(jnp.int32, (SUB, HG_HB * HG_DK), 0)
    lb = lb_ref[0]
    n_chunks = ROW_BLOCK // CHUNK

    def step(i, carry):
        cc = (n_chunks - 1 - i) if rev else i
        sl = pl.ds(pl.multiple_of(cc * CHUNK, CHUNK), CHUNK)
        o, sts = _gla_chunk(q_ref[sl, :].astype(F32), f_ref[sl, :].astype(F32),
                            v_ref[sl, :].astype(F32), lb,
                            [st_ref[h] for h in range(HG_HB)], rev, (tri, ones_kk, trow))
        o_ref[sl, :] = o
        for h in range(HG_HB):
            st_ref[h] = sts[h]
        return carry

    lax.fori_loop(0, n_chunks, step, 0)


def _gla_scan(z, lb, lay, rev):
    m = z.shape[0]
    d_model = HG_HEADS * HG_DK
    ng = HG_HEADS // HG_HB
    width = HG_HB * HG_DK
    f_off = ng * (2 if rev else 1)
    v_off = ng * 3
    rmap = lambda b, c: lay.seq_block(b, c, rev)
    return pl.pallas_call(
        functools.partial(_gla_body, rev=rev),
        grid=(lay.bsz, ng, lay.nlb + lay.ncb),
        in_specs=[
            pl.BlockSpec((ROW_BLOCK, width), lambda b, h, c: (rmap(b, c), h)),
            pl.BlockSpec((ROW_BLOCK, width), lambda b, h, c: (rmap(b, c), f_off + h)),
            pl.BlockSpec((ROW_BLOCK, width), lambda b, h, c: (rmap(b, c), v_off + h)),
            pl.BlockSpec((1, 1, width), lambda b, h, c: (h, 0, 0)),
        ],
        out_specs=pl.BlockSpec((ROW_BLOCK, width), lambda b, h, c: (rmap(b, c), h)),
        out_shape=jax.ShapeDtypeStruct((m, d_model), F32),
        scratch_shapes=[pltpu.VMEM((HG_HB, HG_DK, HG_DK), F32)],
        compiler_params=_params(("arbitrary", "arbitrary", "arbitrary")),
        name="gla_rev" if rev else "gla_fwd",
    )(z, z, z, lb.reshape(ng, 1, width))


def _hg_readout_body(of_ref, ob_ref, g_ref, gw_ref, w_ref, xs_ref, m_ref, ng_ref,
                     xo_ref, ho_ref, wb_ref):
    @pl.when(pl.program_id(0) == 0)
    def _():
        wb_ref[...] = w_ref[...].astype(BF16)

    o = of_ref[...] + ob_ref[...]
    parts = []
    for h in range(HG_HEADS):
        blk = o[:, h * HG_DK:(h + 1) * HG_DK]
        ms = jnp.mean(blk * blk, axis=-1, keepdims=True)
        parts.append(blk * lax.rsqrt(ms + NORM_EPS))
    gate = g_ref[...].astype(F32)
    y = jnp.concatenate(parts, axis=1) * gw_ref[...] * (gate * jax.nn.sigmoid(gate))
    acc = jnp.dot(y.astype(BF16), wb_ref[...], preferred_element_type=F32)
    xs_new = xs_ref[...] + m_ref[0, 2:3] * acc
    xo_ref[...] = xs_new
    ho_ref[...] = _norm_modulate(xs_new, ng_ref[...], m_ref[0, 3:4], m_ref[0, 4:5])


def _hg_readout(o_f, o_b, z, gnorm_w, w_out, xs, modt, norm_g, lay):
    m, d = xs.shape
    row = pl.BlockSpec((ROW_BLOCK, d), lambda i: (i, 0))
    vec = pl.BlockSpec((1, d), lambda i: (0, 0))
    return pl.pallas_call(
        _hg_readout_body,
        grid=(m // ROW_BLOCK,),
        in_specs=[row, row,
                  pl.BlockSpec((ROW_BLOCK, d), lambda i: (i, 4)),
                  vec,
                  pl.BlockSpec((d, d), lambda i: (0, 0)),
                  row,
                  pl.BlockSpec((1,) + modt.shape[1:], lambda i: (lay.segment(i), 0, 0)),
                  vec],
        out_specs=[row, row],
        out_shape=[jax.ShapeDtypeStruct((m, d), F32), jax.ShapeDtypeStruct((m, d), F32)],
        scratch_shapes=[pltpu.VMEM((d, d), BF16)],
        compiler_params=_params(("arbitrary",)),
        name="hg_readout",
    )(o_f, o_b, z, jnp.tile(gnorm_w, HG_HEADS).reshape(1, d), w_out, xs, modt,
      norm_g.reshape(1, d))


RW_HB = 16


def _rwkv_chunk(lw, cum, r, k, v, kap, a, sts, rev, consts):
    incl, strict, eye = consts
    nh = len(sts)
    heads = range(nh)
    hs = [slice(h * RW_DK, (h + 1) * RW_DK) for h in heads]
    cum_ex = cum - lw
    tot = cum[0:1] if rev else cum[CHUNK - 1:CHUNK]
    e_neg = jnp.exp(-cum)
    kt = kap * jnp.exp(cum_ex)
    bb = kap * a
    bt = bb * e_neg
    kd = k * e_neg
    rt = r * jnp.exp(cum)
    dec_end = jnp.exp(tot - cum)
    b_end = bb * dec_end
    k_end = k * dec_end
    e_tot = jnp.exp(tot)

    gram = [_dot_nt(jnp.concatenate([kt[:, s], rt[:, s]], axis=0),
                    jnp.concatenate([bt[:, s], kd[:, s]], axis=0)) for s in hs]
    a_bk = [jnp.where(strict, g[:CHUNK, :CHUNK], 0.0) for g in gram]
    a_kr = [jnp.concatenate([jnp.where(strict, g[:CHUNK, CHUNK:], 0.0),
                             jnp.where(incl, g[CHUNK:, CHUNK:], 0.0)], axis=0) for g in gram]
    a_rb = [jnp.where(incl, g[CHUNK:, :CHUNK], 0.0) for g in gram]
    av = [_dot(a_kr[h], v[:, hs[h]]) for h in heads]

    tinv = [eye - x for x in a_bk]
    pw = [_dot(x, x) for x in a_bk]
    for _ in range(4):
        tp = [_dot(jnp.concatenate([tinv[h], pw[h]], axis=0), pw[h]) for h in heads]
        tinv = [tinv[h] + tp[h][:CHUNK] for h in heads]
        pw = [x[CHUNK:] for x in tp]
    tinv = [tinv[h] + _dot(tinv[h], pw[h]) for h in heads]

    ku = [_dot(tinv[h], jnp.concatenate([kt[:, hs[h]], av[h][:CHUNK]], axis=1)) for h in heads]
    ab = [_dot(a_rb[h], ku[h]) for h in heads]
    rh = [rt[:, hs[h]] - ab[h][:, :RW_DK] for h in heads]
    y0 = [av[h][CHUNK:] - ab[h][:, RW_DK:] for h in heads]
    ys = [_dot_nt(rh[h], sts[h]) + y0[h] for h in heads]
    mb = [_dot(ku[h].T, b_end[:, hs[h]]) for h in heads]
    m_mat = [jnp.where(eye > 0, e_tot[:, hs[h]], 0.0) - mb[h][:RW_DK] for h in heads]
    n_mat = [_dot(v[:, hs[h]].T, k_end[:, hs[h]]) - mb[h][RW_DK:] for h in heads]
    sts_new = [_dot3(sts[h], m_mat[h]) + n_mat[h] for h in heads]
    return jnp.concatenate(ys, axis=1), sts_new


def _rwkv_body(r_ref, k_ref, v_ref, lw_ref, la_ref, w2_ref, w0_ref, a2_ref, a0_ref,
               kk_ref, ka_ref, rk_ref, y_ref, bv_ref, st_ref, *, rev):
    @pl.when(pl.program_id(2) == 0)
    def _():
        st_ref[...] = jnp.zeros_like(st_ref)

    row = lax.broadcasted_iota(jnp.int32, (CHUNK, CHUNK), 0)
    col = lax.broadcasted_iota(jnp.int32, (CHUNK, CHUNK), 1)
    incl = (row <= col) if rev else (row >= col)
    strict = (row < col) if rev else (row > col)
    eye = (row == col).astype(F32)
    tri = incl.astype(F32)
    ones_blk = _head_ones(RW_DK)
    n_chunks = ROW_BLOCK // CHUNK

    def step(i, carry):
        cc = (n_chunks - 1 - i) if rev else i
        sl = pl.ds(pl.multiple_of(cc * CHUNK, CHUNK), CHUNK)
        r, k, v = (ref[sl, :].astype(F32) for ref in (r_ref, k_ref, v_ref))
        lo_w = (1 if rev else 0) * (lw_ref.shape[1] // 2)
        lo_a = (1 if rev else 0) * (la_ref.shape[1] // 2)
        zw = jnp.dot(lw_ref[sl, :][:, lo_w:lo_w + w2_ref.shape[0]], w2_ref[...].astype(BF16),
                     preferred_element_type=F32) + w0_ref[...]
        za = jnp.dot(la_ref[sl, :][:, lo_a:lo_a + a2_ref.shape[0]], a2_ref[...].astype(BF16),
                     preferred_element_type=F32) + a0_ref[...]
        kmul = k * kk_ref[...]
        kap = kmul * lax.rsqrt(jnp.maximum(_head_sums(kmul * kmul, ones_blk), 1e-24))
        a = jax.nn.sigmoid(za)
        lw = -jnp.exp(-jax.nn.softplus(-zw) - 0.5)
        k_dir = k * (1.0 + (a - 1.0) * ka_ref[...])
        cum = jnp.dot(tri, lw, precision=HIGHEST, preferred_element_type=F32)
        y, sts = _rwkv_chunk(lw, cum, r, k_dir, v, kap, a, [st_ref[h] for h in range(RW_HB)],
                             rev, (incl, strict, eye))
        for h in range(RW_HB):
            st_ref[h] = sts[h]
        y_ref[sl, :] = y
        bv_ref[sl, :] = _head_sums(r * k_dir * rk_ref[...], ones_blk) * v
        return carry

    lax.fori_loop(0, n_chunks, step, 0)


def _rwkv_scan(r, k, v, lora_w, lora_a, w2, w0, a2, a0, k_k, k_a, r_k, lay, rev):
    m, d_model = r.shape
    width = RW_HB * RW_DK
    rmap = lambda b, c: lay.seq_block(b, c, rev)
    spec = pl.BlockSpec((ROW_BLOCK, width), lambda b, h, c: (rmap(b, c), h))
    lora = lambda x: pl.BlockSpec((ROW_BLOCK, x.shape[1]), lambda b, h, c: (rmap(b, c), 0))
    mat = lambda x: pl.BlockSpec((x.shape[0], width), lambda b, h, c: (0, h))
    vec = pl.BlockSpec((1, width), lambda b, h, c: (0, h))
    out = jax.ShapeDtypeStruct((m, d_model), F32)
    row = lambda x: x.reshape(1, d_model)
    return pl.pallas_call(
        functools.partial(_rwkv_body, rev=rev),
        grid=(lay.bsz, d_model // width, lay.nlb + lay.ncb),
        in_specs=[spec] * 3 + [lora(lora_w), lora(lora_a), mat(w2), vec, mat(a2), vec] + [vec] * 3,
        out_specs=[spec, spec],
        out_shape=[out, out],
        scratch_shapes=[pltpu.VMEM((RW_HB, RW_DK, RW_DK), F32)],
        compiler_params=_params(("arbitrary", "arbitrary", "arbitrary")),
        name="rwkv_rev" if rev else "rwkv_fwd",
    )(r, k, v, lora_w, lora_a, w2, row(w0), a2, row(a0), row(k_k), row(k_a), row(r_k))


def _rw_readout_body(yf_ref, yb_ref, bf_ref, bb_ref, g_ref, lw_ref, lb_ref, w_ref, xs_ref,
                     m_ref, ng_ref, xo_ref, ho_ref, wb_ref):
    @pl.when(pl.program_id(0) == 0)
    def _():
        wb_ref[...] = w_ref[...].astype(BF16)

    ones_blk = _head_ones(RW_DK)
    y = yf_ref[...] + yb_ref[...]
    dev = y - _head_sums(y, ones_blk) * (1.0 / RW_DK)
    var = _head_sums(dev * dev, ones_blk) * (1.0 / RW_DK)
    yn = dev * lax.rsqrt(var + RW_LNX_EPS) * lw_ref[...] + lb_ref[...]
    yn = yn + bf_ref[...] + bb_ref[...]
    acc = jnp.dot((yn * g_ref[...].astype(F32)).astype(BF16), wb_ref[...],
                  preferred_element_type=F32)
    xs_new = xs_ref[...] + m_ref[0, 2:3] * acc
    xo_ref[...] = xs_new
    ho_ref[...] = _norm_modulate(xs_new, ng_ref[...], m_ref[0, 3:4], m_ref[0, 4:5])


def _rw_readout(y_f, y_b, bv_f, bv_b, gate, lnx_w, lnx_b, w_out, xs, modt, norm_g, lay):
    m, d = xs.shape
    row = pl.BlockSpec((ROW_BLOCK, d), lambda i: (i, 0))
    vec = pl.BlockSpec((1, d), lambda i: (0, 0))
    return pl.pallas_call(
        _rw_readout_body,
        grid=(m // ROW_BLOCK,),
        in_specs=[row] * 5 + [vec, vec, pl.BlockSpec((d, d), lambda i: (0, 0)), row,
                              pl.BlockSpec((1,) + modt.shape[1:],
                                           lambda i: (lay.segment(i), 0, 0)),
                              vec],
        out_specs=[row, row],
        out_shape=[jax.ShapeDtypeStruct((m, d), F32), jax.ShapeDtypeStruct((m, d), F32)],
        scratch_shapes=[pltpu.VMEM((d, d), BF16)],
        compiler_params=_params(("arbitrary",)),
        name="rw_readout",
    )(y_f, y_b, bv_f, bv_b, gate, lnx_w.reshape(1, d), lnx_b.reshape(1, d), w_out, xs, modt,
      norm_g.reshape(1, d))


def _router_body(x_ref, w_ref, b_ref, idx_ref, gate_ref):
    logits = _dot3(x_ref[...], w_ref[...]) + b_ref[...]
    lane = lax.broadcasted_iota(jnp.int32, logits.shape, 1)
    vals, idxs = [], []
    for _ in range(TOP_K):
        m = jnp.max(logits, axis=-1, keepdims=True)
        idx = jnp.min(jnp.where(logits == m, lane, N_EXPERTS), axis=-1, keepdims=True)
        vals.append(m)
        idxs.append(idx)
        logits = jnp.where(lane == idx, -jnp.inf, logits)
    exps = [jnp.exp(val - vals[0]) for val in vals]
    denom = exps[0] + exps[1] + exps[2] + exps[3]
    idx_ref[...] = jnp.concatenate(idxs, axis=1)
    gate_ref[...] = jnp.concatenate(exps, axis=1) / denom


def _router(x, n, w, b):
    d = x.shape[1]
    return pl.pallas_call(
        _router_body,
        grid=(n // ROW_BLOCK,),
        in_specs=[
            pl.BlockSpec((ROW_BLOCK, d), lambda i: (i, 0)),
            pl.BlockSpec((d, N_EXPERTS), lambda i: (0, 0)),
            pl.BlockSpec((1, N_EXPERTS), lambda i: (0, 0)),
        ],
        out_specs=[pl.BlockSpec((ROW_BLOCK, TOP_K), lambda i: (i, 0)),
                   pl.BlockSpec((ROW_BLOCK, TOP_K), lambda i: (i, 0))],
        out_shape=[jax.ShapeDtypeStruct((n, TOP_K), jnp.int32),
                   jax.ShapeDtypeStruct((n, TOP_K), F32)],
        compiler_params=_params(("arbitrary",)),
        name="router",
    )(x, w, b.reshape(1, N_EXPERTS))


EXPERT_COLS = 512


def _expert_body(be_ref, nu_ref, x_ref, wgu_ref, bgu_ref, wd_ref, bd_ref, o_ref,
                 wgu_b, wd_b):
    i = pl.program_id(0)
    d_exp = wd_ref.shape[1]
    prev = be_ref[jnp.maximum(i - 1, 0)]

    @pl.when((i == 0) | (be_ref[i] != prev))
    def _():
        wgu_b[...] = wgu_ref[0].astype(BF16)
        wd_b[...] = wd_ref[0].astype(BF16)

    @pl.when(i < nu_ref[0])
    def _():
        x = x_ref[...].astype(BF16)
        acc = jnp.zeros(o_ref.shape, F32)
        for c0 in range(0, d_exp, EXPERT_COLS):
            gate = jnp.dot(x, wgu_b[:, c0:c0 + EXPERT_COLS], preferred_element_type=F32)
            gate = gate + bgu_ref[0, :, c0:c0 + EXPERT_COLS]
            up = jnp.dot(x, wgu_b[:, d_exp + c0:d_exp + c0 + EXPERT_COLS],
                         preferred_element_type=F32)
            up = up + bgu_ref[0, :, d_exp + c0:d_exp + c0 + EXPERT_COLS]
            gate = jnp.minimum(gate, SWIGLU_LIMIT)
            up = jnp.clip(up, -SWIGLU_LIMIT, SWIGLU_LIMIT)
            act = gate * jax.nn.sigmoid(SWIGLU_ALPHA * gate) * (up + 1.0)
            acc = acc + jnp.dot(act.astype(BF16), wd_b[c0:c0 + EXPERT_COLS, :],
                                preferred_element_type=F32)
        o_ref[...] = acc + bd_ref[0]

    @pl.when(i >= nu_ref[0])
    def _():
        o_ref[...] = jnp.zeros_like(o_ref)


def _experts(xs, block_e, n_used, w_gu, b_gu, w_down, b_down):
    cap, d = xs.shape
    n_e, _, d2 = w_gu.shape
    d_exp = d2 // 2
    nb = cap // MOE_BLOCK
    grid_spec = pltpu.PrefetchScalarGridSpec(
        num_scalar_prefetch=2,
        grid=(nb,),
        in_specs=[
            pl.BlockSpec((MOE_BLOCK, d), lambda i, be, nu: (i, 0)),
            pl.BlockSpec((1, d, d2), lambda i, be, nu: (be[i], 0, 0)),
            pl.BlockSpec((1, 1, d2), lambda i, be, nu: (be[i], 0, 0)),
            pl.BlockSpec((1, d_exp, d), lambda i, be, nu: (be[i], 0, 0)),
            pl.BlockSpec((1, 1, d), lambda i, be, nu: (be[i], 0, 0)),
        ],
        out_specs=pl.BlockSpec((MOE_BLOCK, d), lambda i, be, nu: (i, 0)),
        scratch_shapes=[pltpu.VMEM((d, d2), BF16), pltpu.VMEM((d_exp, d), BF16)],
    )
    return pl.pallas_call(
        _expert_body,
        grid_spec=grid_spec,
        out_shape=jax.ShapeDtypeStruct((cap, d), F32),
        compiler_params=_params(("arbitrary",)),
        name="experts",
    )(block_e, n_used, xs, w_gu, b_gu.reshape(n_e, 1, d2), w_down, b_down.reshape(n_e, 1, d))


def _row_copies(make_copy, start):
    def body(r, carry):
        for j in range(TOP_K):
            cp = make_copy(r, j)
            cp.start(priority=j % 2) if start else cp.wait()
        return carry

    lax.fori_loop(0, ROW_BLOCK, body, 0, unroll=8)


def _dispatch_body(pad_ref, dst_ref, x_ref, xs_ref, zeros, sem, zsem):
    @pl.when(pl.program_id(0) == 0)
    def _():
        zeros[...] = jnp.zeros_like(zeros)
        for e in range(N_EXPERTS):
            start = pl.multiple_of(pad_ref[e], SUBLANES)
            fill = pltpu.make_async_copy(zeros, xs_ref.at[pl.ds(start, PAD_FILL)], zsem)
            fill.start()
            fill.wait()

    def copy(r, j):
        return pltpu.make_async_copy(x_ref.at[pl.ds(r, 1)],
                                     xs_ref.at[pl.ds(dst_ref[0, 0, r * TOP_K + j], 1)], sem)

    _row_copies(copy, True)
    _row_copies(copy, False)


def _dispatch(tokens, n, dest, pad_start, cap):
    d = tokens.shape[1]
    nblk = n // ROW_BLOCK
    grid_spec = pltpu.PrefetchScalarGridSpec(
        num_scalar_prefetch=1,
        grid=(nblk,),
        in_specs=[
            pl.BlockSpec((1, 1, ROW_BLOCK * TOP_K), lambda i, pad: (i, 0, 0),
                         memory_space=pltpu.SMEM),
            pl.BlockSpec((ROW_BLOCK, d), lambda i, pad: (i, 0)),
        ],
        out_specs=pl.BlockSpec(memory_space=pl.ANY),
        scratch_shapes=[pltpu.VMEM((PAD_FILL, d), F32), pltpu.SemaphoreType.DMA(()),
                        pltpu.SemaphoreType.DMA(())],
    )
    return pl.pallas_call(
        _dispatch_body,
        grid_spec=grid_spec,
        out_shape=jax.ShapeDtypeStruct((cap, d), F32),
        compiler_params=_params(("arbitrary",)),
        name="dispatch",
    )(pad_start, dest.reshape(nblk, 1, ROW_BLOCK * TOP_K), tokens)


def _combine_body(dst_ref, gate_ref, xs_ref, m_ref, ng_ref, m2_ref, ys_ref, xo_ref, ho_ref,
                  buf, sem):
    def copy(r, j):
        return pltpu.make_async_copy(ys_ref.at[pl.ds(dst_ref[0, 0, r * TOP_K + j], 1)],
                                     buf.at[j, pl.ds(r, 1)], sem)

    _row_copies(copy, True)
    _row_copies(copy, False)
    gates = gate_ref[...]
    acc = gates[:, 0:1] * buf[0]
    for j in range(1, TOP_K):
        acc = acc + gates[:, j:j + 1] * buf[j]
    xs_new = xs_ref[...] + m_ref[0, 5:6] * acc
    xo_ref[...] = xs_new
    ho_ref[...] = _norm_modulate(xs_new, ng_ref[...], m2_ref[0, 0:1], m2_ref[0, 1:2])


def _combine(ys, dest, gates, xs, modt, next_g, next_modt, lay):
    n = gates.shape[0]
    d = ys.shape[1]
    nblk = n // ROW_BLOCK
    row = pl.BlockSpec((ROW_BLOCK, d), lambda i: (i, 0))
    seg = lambda mt: pl.BlockSpec((1,) + mt.shape[1:], lambda i: (lay.segment(i), 0, 0))
    return pl.pallas_call(
        _combine_body,
        grid=(nblk,),
        in_specs=[
            pl.BlockSpec((1, 1, ROW_BLOCK * TOP_K), lambda i: (i, 0, 0),
                         memory_space=pltpu.SMEM),
            pl.BlockSpec((ROW_BLOCK, TOP_K), lambda i: (i, 0)),
            row, seg(modt), pl.BlockSpec((1, d), lambda i: (0, 0)), seg(next_modt),
            pl.BlockSpec(memory_space=pl.ANY),
        ],
        out_specs=[row, row],
        out_shape=[jax.ShapeDtypeStruct((n, d), F32), jax.ShapeDtypeStruct((n, d), F32)],
        scratch_shapes=[pltpu.VMEM((TOP_K, ROW_BLOCK, d), F32), pltpu.SemaphoreType.DMA(())],
        compiler_params=_params(("arbitrary",)),
        name="combine",
    )(dest.reshape(nblk, 1, ROW_BLOCK * TOP_K), gates, xs, modt, next_g.reshape(1, d),
      next_modt, ys)


def _moe(h, n, xs, modt, next_g, next_modt, lay, router_w, router_b, w_gu, b_gu, w_down, b_down):
    d = h.shape[1]
    top_idx, gates = _router(h, n, router_w, router_b)
    flat_e = top_idx.reshape(-1)
    onehot = (flat_e[:, None] == jnp.arange(N_EXPERTS, dtype=jnp.int32)[None, :]).astype(jnp.int32)
    rank = jnp.sum((jnp.cumsum(onehot, axis=0) - 1) * onehot, axis=1)
    counts = jnp.sum(onehot, axis=0)
    padded = (counts + MOE_BLOCK - 1) // MOE_BLOCK * MOE_BLOCK
    padded_end = jnp.cumsum(padded)
    padded_start = padded_end - padded
    dest = jnp.sum(onehot * padded_start[None, :], axis=1) + rank
    n_blocks = -(-(n * TOP_K + N_EXPERTS * (MOE_BLOCK - 1)) // MOE_BLOCK)
    cap = n_blocks * MOE_BLOCK
    n_used = (padded_end[-1] // MOE_BLOCK).astype(jnp.int32)
    blk = jnp.minimum(jnp.arange(n_blocks, dtype=jnp.int32), n_used - 1) * MOE_BLOCK
    block_e = jnp.minimum(jnp.sum((padded_end[None, :] <= blk[:, None]).astype(jnp.int32), axis=1),
                          N_EXPERTS - 1)
    pad_start = jnp.minimum((padded_start + counts) // SUBLANES * SUBLANES,
                            cap - PAD_FILL).astype(jnp.int32)
    slots = _dispatch(h, n, dest, pad_start, cap)
    ys = _experts(slots, block_e, n_used.reshape(1), w_gu, b_gu, w_down, b_down)
    return _combine(ys, dest, gates, xs, modt, next_g, next_modt, lay)


def _shift_lerp_body(hp_ref, h_ref, hn_ref, mu_ref, *o_refs, bsz, nlb, ncb):
    i = pl.program_id(0)
    x = h_ref[...]
    d = x.shape[1]
    q, half, last = d // 4, d // 2, ROW_BLOCK - 1
    row = lax.broadcasted_iota(jnp.int32, (ROW_BLOCK, 1), 0)
    col = row % GRID_W
    lat_blk = i % nlb
    left = jnp.where(col == 0, 0.0, pltpu.roll(x[:, :q], 1, 0))
    right = jnp.where(col == GRID_W - 1, 0.0, pltpu.roll(x[:, q:2 * q], last, 0))
    up = jnp.concatenate([hp_ref[ROW_BLOCK - GRID_W:, 2 * q:3 * q],
                          x[:ROW_BLOCK - GRID_W, 2 * q:3 * q]], axis=0)
    up = jnp.where((row < GRID_W) & (lat_blk == 0), 0.0, up)
    down = jnp.concatenate([x[GRID_W:, 3 * q:], hn_ref[:GRID_W, 3 * q:]], axis=0)
    down = jnp.where((row >= ROW_BLOCK - GRID_W) & (lat_blk == nlb - 1), 0.0, down)
    shift_lat = jnp.concatenate([left, right, up, down], axis=1)

    ctx_blk = (i - bsz * nlb) % ncb
    prev_edge = jnp.where(ctx_blk == 0, 0.0, hp_ref[last:, :half])
    prev = jnp.where(row == 0, prev_edge, pltpu.roll(x[:, :half], 1, 0))
    next_edge = jnp.where(ctx_blk == ncb - 1, 0.0, hn_ref[0:1, half:])
    nxt = jnp.where(row == last, next_edge, pltpu.roll(x[:, half:], last, 0))
    shift_ctx = jnp.concatenate([prev, nxt], axis=1)

    dx = jnp.where(i < bsz * nlb, shift_lat, shift_ctx) - x
    for j, o_ref in enumerate(o_refs):
        o_ref[...] = (x + dx * mu_ref[j:j + 1]).astype(o_ref.dtype)


def _shift_lerp(h, mu, lay):
    assert ROW_BLOCK % GRID_W == 0 and lay.n_lat % GRID_W == 0
    m, d = h.shape
    nblk = m // ROW_BLOCK
    n_out = mu.shape[0]
    row = pl.BlockSpec((ROW_BLOCK, d), lambda i: (i, 0))
    return pl.pallas_call(
        functools.partial(_shift_lerp_body, bsz=lay.bsz, nlb=lay.nlb, ncb=lay.ncb),
        grid=(nblk,),
        in_specs=[pl.BlockSpec((ROW_BLOCK, d), lambda i: (jnp.maximum(i - 1, 0), 0)),
                  row,
                  pl.BlockSpec((ROW_BLOCK, d), lambda i: (jnp.minimum(i + 1, nblk - 1), 0)),
                  pl.BlockSpec((n_out, d), lambda i: (0, 0))],
        out_specs=[row] * n_out,
        out_shape=[jax.ShapeDtypeStruct((m, d), BF16)] * n_out,
        compiler_params=_params(("arbitrary",)),
        name="shift_lerp",
    )(h, h, h, mu)


def _rwkv7_mixer(h, xs, modt, norm_g, lay, mu, w_rkv, dec_w0, dec_w1, dec_w2, iclr_a0, iclr_a1,
                 iclr_a2, g1, g2, k_k, k_a, r_k, lnx_w, lnx_b, w_out):
    xr, xw, xk, xv, xa, xg = _shift_lerp(h, mu, lay)
    r = _mm(xr, w_rkv[0], out_dtype=BF16)
    k = _mm(xk, w_rkv[1], out_dtype=BF16)
    v = _mm(xv, w_rkv[2], out_dtype=BF16)
    lora_w = _mm(xw, jnp.concatenate([dec_w1[0], dec_w1[1]], axis=1), act="tanh", out_dtype=BF16)
    lora_a = _mm(xa, jnp.concatenate([iclr_a1[0], iclr_a1[1]], axis=1), out_dtype=BF16)
    gate = _mm(_mm(xg, g1, act="sigmoid", out_dtype=BF16), g2, out_dtype=BF16)
    ys = []
    for dd, rev in enumerate((False, True)):
        ys.append(_rwkv_scan(r, k, v, lora_w, lora_a, dec_w2[dd], dec_w0[dd], iclr_a2[dd],
                             iclr_a0[dd], k_k, k_a, r_k, lay, rev))
    return _rw_readout(ys[0][0], ys[1][0], ys[0][1], ys[1][1], gate, lnx_w, lnx_b, w_out,
                       xs, modt, norm_g, lay)


def kernel(x, c, ctx, c_ctx, ada_w, ada_b, norm_mix_g, norm_ffn_g, hg_w_in, hg_gnorm_w, hg_w_out, hg_lb, rw_mu, rw_w_rkv, rw_dec_w0, rw_dec_w1, rw_dec_w2, rw_iclr_a0, rw_iclr_a1, rw_iclr_a2, rw_g1, rw_g2, rw_k_k, rw_k_a, rw_r_k, rw_lnx_w, rw_lnx_b, rw_w_out, moe_router_w, moe_router_b, moe_w_gu, moe_b_gu, moe_w_down, moe_b_down, final_g):
    bsz, n_lat, d = x.shape
    n_ctx = ctx.shape[1]
    depth = ada_w.shape[0]
    lay = _Layout(bsz, n_lat, n_ctx)
    lb_all = jnp.cumsum(jax.nn.softmax(hg_lb.astype(F32), axis=1), axis=1)
    cond = jnp.concatenate([jax.nn.silu(c), jax.nn.silu(c_ctx)[None, :],
                            jnp.zeros((8 - bsz - 1, d), F32)], axis=0)
    modts = [_mm_exact(cond, ada_w[l], ada_b[l])[:bsz + 1].reshape(bsz + 1, 6, d)
             for l in range(depth)]
    xs = jnp.concatenate([x.reshape(bsz * n_lat, d), ctx.reshape(bsz * n_ctx, d)], axis=0)
    h = _norm_mod(xs, norm_mix_g[0], modts[0], lay, BF16)
    n_rows = lay.rows
    for layer in range(depth):
        last = layer == depth - 1
        modt = modts[layer]
        j = layer // 2
        if layer % 2 == 0:
            z = _mm(h.astype(BF16), hg_w_in[j], out_dtype=BF16)
            lb = lb_all[:, layer]
            o_f = _gla_scan(z, lb[0], lay, False)
            o_b = _gla_scan(z, lb[1], lay, True)
            xs, h = _hg_readout(o_f, o_b, z, hg_gnorm_w[j], hg_w_out[j], xs, modt,
                                norm_ffn_g[layer], lay)
        else:
            xs, h = _rwkv7_mixer(h, xs, modt, norm_ffn_g[layer], lay, rw_mu[j], rw_w_rkv[j],
                                 rw_dec_w0[j], rw_dec_w1[j], rw_dec_w2[j], rw_iclr_a0[j],
                                 rw_iclr_a1[j], rw_iclr_a2[j], rw_g1[j], rw_g2[j], rw_k_k[j],
                                 rw_k_a[j], rw_r_k[j], rw_lnx_w[j], rw_lnx_b[j], rw_w_out[j])
        if last:
            n_rows = bsz * n_lat
            next_g, next_modt = final_g, jnp.zeros_like(modt)
        else:
            next_g, next_modt = norm_mix_g[layer + 1], modts[layer + 1]
        xs, h = _moe(h, n_rows, xs, modt, next_g, next_modt, lay, moe_router_w[layer],
                     moe_router_b[layer], moe_w_gu[layer], moe_b_gu[layer], moe_w_down[layer],
                     moe_b_down[layer])
    return h.reshape(bsz, n_lat, d)
```
